```python
import jax, jax.numpy as jnp
from jax import lax
import numpy as np

D_MODEL = 1024
BATCH = 2
SEQ = 8192
DEPTH = 4
DEC_BATCH = 128
DEC_SEQ = 1
PAST_LEN = 8192
PAGE_SIZE = 128

HEAD_DIM = 64
N_HEADS = 8
N_KV_HEADS = 2
GQA_GROUP = N_HEADS // N_KV_HEADS
WINDOW = 128
D_ATTN = N_HEADS * HEAD_DIM
D_KV = N_KV_HEADS * HEAD_DIM
D_LRU = 256
N_LRU_BLOCKS = 4
LRU_BLOCK = D_LRU // N_LRU_BLOCKS
LRU_CONV = 4
LRU_C = 8.0
D_SC = 256
SC_CONV = 3
D_MIX = D_ATTN + D_LRU + D_SC
D_IN = D_ATTN + 2 * D_KV + 2 * D_LRU + 3 * D_SC
D_FF = 2816
FFN_CONV = 3
EPS = 1e-6

kernel_name = "hymba_style_swa_rglru_shortconv_decoder_step"


def rmsnorm(x, g):
    xf = x.astype(jnp.float32)
    y = xf * lax.rsqrt(jnp.mean(xf * xf, axis=-1, keepdims=True) + EPS)
    return (y * g.astype(jnp.float32)).astype(x.dtype)


def causal_conv(buf, u, w):
    T = u.shape[1]
    K = w.shape[0]
    xp = jnp.concatenate([buf, u], axis=1)
    y = xp[:, 0:T] * w[0]
    for k in range(1, K):
        y = y + xp[:, k:k + T] * w[k]
    return y, xp[:, T:]


def sink_softmax(s, sink):
    m = jnp.maximum(jnp.max(s, axis=-1, keepdims=True), sink)
    p = jnp.exp(s - m)
    return p / (jnp.sum(p, axis=-1, keepdims=True) + jnp.exp(sink - m))


def attn_prompt(q, k, v, sinks):
    B, T, _ = q.shape
    nb = T // WINDOW
    qb = q.reshape(B, nb, WINDOW, N_KV_HEADS, GQA_GROUP, HEAD_DIM)
    kb = k.reshape(B, nb, WINDOW, N_KV_HEADS, HEAD_DIM)
    vb = v.reshape(B, nb, WINDOW, N_KV_HEADS, HEAD_DIM)
    pad = ((0, 0), (1, 0), (0, 0), (0, 0), (0, 0))
    kk = jnp.concatenate([jnp.pad(kb, pad)[:, :-1], kb], axis=2)
    vv = jnp.concatenate([jnp.pad(vb, pad)[:, :-1], vb], axis=2)
    s = jnp.einsum('bnqkgd,bnskd->bnkgqs', qb, kk).astype(jnp.float32) * (HEAD_DIM ** -0.5)
    qi = jnp.arange(WINDOW)[:, None] + WINDOW
    kj = jnp.arange(2 * WINDOW)[None, :]
    diff = qi - kj
    band = (diff >= 0) & (diff <= WINDOW)
    first = (jnp.arange(nb)[:, None, None] == 0) & (kj[None] < WINDOW)
    mask = band[None] & ~first
    s = jnp.where(mask[None, :, None, None], s, -jnp.inf)
    sink = sinks.reshape(N_KV_HEADS, GQA_GROUP)[:, :, None, None].astype(jnp.float32)
    p = sink_softmax(s, sink).astype(v.dtype)
    o = jnp.einsum('bnkgqs,bnskd->bnqkgd', p, vv).reshape(B, T, D_ATTN)
    k_new = k[:, -WINDOW:].reshape(B, WINDOW, N_KV_HEADS, HEAD_DIM)
    v_new = v[:, -WINDOW:].reshape(B, WINDOW, N_KV_HEADS, HEAD_DIM)
    return o, k_new, v_new


def attn_sample(q, k, v, cache_k, cache_v, sinks):
    B, T, _ = q.shape
    W = cache_k.shape[1]
    kk = jnp.concatenate([cache_k, k.reshape(B, T, N_KV_HEADS, HEAD_DIM)], axis=1)
    vv = jnp.concatenate([cache_v, v.reshape(B, T, N_KV_HEADS, HEAD_DIM)], axis=1)
    qg = q.reshape(B, T, N_KV_HEADS, GQA_GROUP, HEAD_DIM)
    s = jnp.einsum('btkgd,bskd->bkgts', qg, kk).astype(jnp.float32) * (HEAD_DIM ** -0.5)
    diff = (W + jnp.arange(T))[:, None] - jnp.arange(W + T)[None, :]
    mask = (diff >= 0) & (diff <= WINDOW)
    s = jnp.where(mask, s, -jnp.inf)
    sink = sinks.reshape(N_KV_HEADS, GQA_GROUP)[:, :, None, None].astype(jnp.float32)
    p = sink_softmax(s, sink).astype(v.dtype)
    o = jnp.einsum('bkgts,bskd->btkgd', p, vv).reshape(B, T, D_ATTN)
    return o, kk[:, -W:], vv[:, -W:]


def rglru(xc, h0, wa, ba, wi, bi, lam):
    B, T, _ = xc.shape
    xh = xc.reshape(B, T, N_LRU_BLOCKS, LRU_BLOCK)
    r = jax.nn.sigmoid(jnp.einsum('btnc,ncd->btnd', xh, wa).reshape(B, T, D_LRU) + ba)
    i = jax.nn.sigmoid(jnp.einsum('btnc,ncd->btnd', xh, wi).reshape(B, T, D_LRU) + bi)
    log_a = -LRU_C * r * jax.nn.softplus(-lam)
    a = jnp.exp(log_a)
    bx = jnp.sqrt(-jnp.expm1(2.0 * log_a)) * (i * xc)
    bx = bx.at[:, 0].add(a[:, 0] * h0)

    def combine(left, right):
        a1, b1 = left
        a2, b2 = right
        return a1 * a2, a2 * b1 + b2

    _, h = lax.associative_scan(combine, (a, bx), axis=1)
    return h, h[:, -1]


def decoder_layer(x, attn_cache, h0, lru_buf, sc_buf, ffn_buf,
                  g_mix, w_in, sinks, lru_conv_w, lru_conv_b, lru_wa, lru_ba, lru_wi, lru_bi,
                  lru_lambda, sc_conv_w, g_out, w_out, g_ffn, w_up, ffn_conv_w, w_down):
    h = rmsnorm(x, g_mix)
    z = h @ w_in
    cuts = np.cumsum([D_ATTN, D_KV, D_KV, D_LRU, D_LRU, D_SC, D_SC]).tolist()
    q, k, v, lru_x, lru_g, sc_b, sc_c, sc_u = jnp.split(z, cuts, axis=-1)
    if attn_cache is None:
        a_out, k_new, v_new = attn_prompt(q, k, v, sinks)
    else:
        a_out, k_new, v_new = attn_sample(q, k, v, attn_cache[0], attn_cache[1], sinks)
    xc, lru_buf_new = causal_conv(lru_buf, lru_x, lru_conv_w)
    hseq, h_last = rglru(xc + lru_conv_b, h0, lru_wa, lru_ba, lru_wi, lru_bi, lru_lambda)
    b_out = hseq * jax.nn.gelu(lru_g)
    sc_conv, sc_buf_new = causal_conv(sc_buf, sc_c * sc_u, sc_conv_w)
    c_out = sc_b * sc_conv
    merged = jnp.concatenate([
        rmsnorm(a_out, g_out[:D_ATTN]),
        rmsnorm(b_out, g_out[D_ATTN:D_ATTN + D_LRU]),
        rmsnorm(c_out, g_out[D_ATTN + D_LRU:])], axis=-1)
    x = x + merged @ w_out
    up = rmsnorm(x, g_ffn) @ w_up
    upc, ffn_buf_new = causal_conv(ffn_buf, up, ffn_conv_w)
    u, gt = jnp.split(upc, 2, axis=-1)
    x = x + (jax.nn.silu(gt) * u) @ w_down
    return x, (k_new, v_new, h_last, lru_buf_new, sc_buf_new, ffn_buf_new)


def setup_inputs(seed: int = 0) -> dict:
    key = jax.random.key(seed)
    ks = jax.random.split(key, 32)
    nrm = jax.random.normal
    f32 = jnp.float32
    win = min(WINDOW, PAST_LEN)
    u = jax.random.uniform(ks[20], (DEPTH, D_LRU), f32, 0.9, 0.999)
    s = u ** (1.0 / LRU_C)
    lam = jnp.log(s) - jnp.log1p(-s)
    return {
        "x_prompt": nrm(ks[0], (BATCH, SEQ, D_MODEL), f32),
        "x_sample": nrm(ks[1], (DEC_BATCH, DEC_SEQ, D_MODEL), f32),
        "cache_k": nrm(ks[2], (DEPTH, DEC_BATCH, win, N_KV_HEADS, HEAD_DIM), f32),
        "cache_v": nrm(ks[3], (DEPTH, DEC_BATCH, win, N_KV_HEADS, HEAD_DIM), f32),
        "state_rglru": 0.5 * nrm(ks[4], (DEPTH, DEC_BATCH, D_LRU), f32),
        "state_lru_conv": 0.5 * nrm(ks[5], (DEPTH, DEC_BATCH, LRU_CONV - 1, D_LRU), f32),
        "state_sconv": 0.5 * nrm(ks[6], (DEPTH, DEC_BATCH, SC_CONV - 1, D_SC), f32),
        "state_ffn_conv": 0.5 * nrm(ks[7], (DEPTH, DEC_BATCH, FFN_CONV - 1, 2 * D_FF), f32),
        "g_mix": 1.0 + 0.02 * nrm(ks[8], (DEPTH, D_MODEL), f32),
        "w_in": nrm(ks[9], (DEPTH, D_MODEL, D_IN), f32) * D_MODEL ** -0.5,
        "sinks": 0.5 * nrm(ks[10], (DEPTH, N_HEADS), f32),
        "lru_conv_w": nrm(ks[11], (DEPTH, LRU_CONV, D_LRU), f32) * LRU_CONV ** -0.5,
        "lru_conv_b": 0.01 * nrm(ks[12], (DEPTH, D_LRU), f32),
        "lru_wa": nrm(ks[13], (DEPTH, N_LRU_BLOCKS, LRU_BLOCK, LRU_BLOCK), f32) * LRU_BLOCK ** -0.5,
        "lru_ba": 0.01 * nrm(ks[14], (DEPTH, D_LRU), f32),
        "lru_wi": nrm(ks[15], (DEPTH, N_LRU_BLOCKS, LRU_BLOCK, LRU_BLOCK), f32) * LRU_BLOCK ** -0.5,
        "lru_bi": 0.01 * nrm(ks[16], (DEPTH, D_LRU), f32),
        "lru_lambda": lam,
        "sc_conv_w": nrm(ks[17], (DEPTH, SC_CONV, D_SC), f32) * SC_CONV ** -0.5,
        "g_out": 1.0 + 0.02 * nrm(ks[18], (DEPTH, D_MIX), f32),
        "w_out": nrm(ks[19], (DEPTH, D_MIX, D_MODEL), f32) * D_MIX ** -0.5,
        "g_ffn": 1.0 + 0.02 * nrm(ks[21], (DEPTH, D_MODEL), f32),
        "w_up": nrm(ks[22], (DEPTH, D_MODEL, 2 * D_FF), f32) * D_MODEL ** -0.5,
        "ffn_conv_w": nrm(ks[23], (DEPTH, FFN_CONV, 2 * D_FF), f32) * FFN_CONV ** -0.5,
        "w_down": nrm(ks[24], (DEPTH, D_FF, D_MODEL), f32) * D_FF ** -0.5,
        "g_final": 1.0 + 0.02 * nrm(ks[25], (D_MODEL,), f32),
    }


def _stack(states, i):
    return jnp.stack([s[i] for s in states])


def reference(x_prompt, x_sample, cache_k, cache_v, state_rglru, state_lru_conv, state_sconv,
              state_ffn_conv, g_mix, w_in, sinks, lru_conv_w, lru_conv_b, lru_wa, lru_ba,
              lru_wi, lru_bi, lru_lambda, sc_conv_w, g_out, w_out, g_ffn, w_up, ffn_conv_w,
              w_down, g_final):
    B = x_prompt.shape[0]
    dt = x_prompt.dtype
    h0_p = jnp.zeros((B, D_LRU), dt)
    lru_buf_p = jnp.zeros((B, LRU_CONV - 1, D_LRU), dt)
    sc_buf_p = jnp.zeros((B, SC_CONV - 1, D_SC), dt)
    ffn_buf_p = jnp.zeros((B, FFN_CONV - 1, 2 * D_FF), dt)
    xp, xs = x_prompt, x_sample
    p_states, s_states = [], []
    for l in range(DEPTH):
        lw = (g_mix[l], w_in[l], sinks[l], lru_conv_w[l], lru_conv_b[l], lru_wa[l], lru_ba[l],
              lru_wi[l], lru_bi[l], lru_lambda[l], sc_conv_w[l], g_out[l], w_out[l], g_ffn[l],
              w_up[l], ffn_conv_w[l], w_down[l])
        xp, sp = decoder_layer(xp, None, h0_p, lru_buf_p, sc_buf_p, ffn_buf_p, *lw)
        xs, ss = decoder_layer(xs, (cache_k[l], cache_v[l]), state_rglru[l], state_lru_conv[l],
                               state_sconv[l], state_ffn_conv[l], *lw)
        p_states.append(sp)
        s_states.append(ss)
    y_prompt = rmsnorm(xp, g_final)
    y_sample = rmsnorm(xs, g_final)
    new_cache_k_prompt = _stack(p_states, 0)
    new_cache_v_prompt = _stack(p_states, 1)
    new_state_rglru_prompt = _stack(p_states, 2)
    new_state_lru_conv_prompt = _stack(p_states, 3)
    new_state_sconv_prompt = _stack(p_states, 4)
    new_state_ffn_conv_prompt = _stack(p_states, 5)
    new_cache_k_sample = _stack(s_states, 0)
    new_cache_v_sample = _stack(s_states, 1)
    new_state_rglru_sample = _stack(s_states, 2)
    new_state_lru_conv_sample = _stack(s_states, 3)
    new_state_sconv_sample = _stack(s_states, 4)
    new_state_ffn_conv_sample = _stack(s_states, 5)
    return (y_prompt, y_sample,
            new_cache_k_prompt, new_cache_v_prompt, new_state_rglru_prompt,
            new_state_lru_conv_prompt, new_state_sconv_prompt, new_state_ffn_conv_prompt,
            new_cache_k_sample, new_cache_v_sample, new_state_rglru_sample,
            new_state_lru_conv_sample, new_state_sconv_sample, new_state_ffn_conv_sample)
```

```python
import functools

import jax
import jax.numpy as jnp
from jax import lax
from jax.experimental import pallas as pl
from jax.experimental.pallas import tpu as pltpu

D_MODEL = 1024
DEPTH = 4
HEAD_DIM = 64
N_HEADS = 8
N_KV_HEADS = 2
GQA_GROUP = N_HEADS // N_KV_HEADS
WINDOW = 128
D_ATTN = N_HEADS * HEAD_DIM
D_KV = N_KV_HEADS * HEAD_DIM
D_LRU = 256
N_LRU_BLOCKS = 4
LRU_BLOCK = D_LRU // N_LRU_BLOCKS
LRU_CONV = 4
LRU_C = 8.0
D_SC = 256
SC_CONV = 3
D_MIX = D_ATTN + D_LRU + D_SC
D_IN = D_ATTN + 2 * D_KV + 2 * D_LRU + 3 * D_SC
D_FF = 2816
FFN_CONV = 3
EPS = 1e-6

_Q0, _K0, _V0 = 0, D_ATTN, D_ATTN + D_KV
_LX0 = D_ATTN + 2 * D_KV
_LG0 = _LX0 + D_LRU
_SB0 = _LG0 + D_LRU
_SC0 = _SB0 + D_SC
_SU0 = _SC0 + D_SC

SUBLANES = 8
TOKEN_BLOCK = 512
FF_CHUNK = 256
N_FF_CHUNKS = D_FF // FF_CHUNK
SEQ_GROUP = 8
VMEM_LIMIT_BYTES = 56 * 1024 * 1024

_F32 = jnp.float32
_BF16 = jnp.bfloat16
_NT = (((1,), (1,)), ((), ()))


def _rms(x, g):
    ms = jnp.mean(x * x, axis=-1, keepdims=True)
    return x * lax.rsqrt(ms + EPS) * g


def _softplus(x):
    return jnp.maximum(x, 0.0) + jnp.log1p(jnp.exp(-jnp.abs(x)))


def _lru_coeffs(xc, gates, lam):
    r = jax.nn.sigmoid(gates[:, :D_LRU])
    i = jax.nn.sigmoid(gates[:, D_LRU:])
    log_a = (-LRU_C) * r * _softplus(-lam)
    a = jnp.exp(log_a)
    b = jnp.sqrt(jnp.tanh(-log_a) * (1.0 + a * a)) * (i * xc)
    return a, b


def _scan_rows(a, b):
    n = a.shape[0]
    rows = lax.broadcasted_iota(jnp.int32, a.shape, 0)
    d = 1
    while d < SUBLANES:
        keep = rows >= d
        a_prev = jnp.where(keep, pltpu.roll(a, d, 0), 1.0)
        b_prev = jnp.where(keep, pltpu.roll(b, d, 0), 0.0)
        b = a * b_prev + b
        a = a * a_prev
        d *= 2
    while d < n:
        b = jnp.concatenate([b[:d], a[d:] * b[:-d] + b[d:]], axis=0)
        a = jnp.concatenate([a[:d], a[d:] * a[:-d]], axis=0)
        d *= 2
    return a, b


def _prompt_layer_kernel(
        sinks_ref, x_ref, g_mix_ref, w_in_ref, lcw_ref, lcb_ref, w_gates_ref, b_gates_ref,
        lam_ref, scw_ref, g_out_ref, w_out_ref, g_ffn_ref, w_up_ref, fcw_ref, w_down_ref,
        g_final_ref,
        y_ref, kv_out_ref, h_out_ref, lbuf_out_ref, sbuf_out_ref, fbuf_out_ref,
        kv_scr, lc_scr, sc_scr, h_scr, up_scr, fhist_scr, acc_scr, hn_scr,
        *, final):
    tb = TOKEN_BLOCK
    t = pl.program_id(1)

    @pl.when(t == 0)
    def _():
        kv_scr[0:WINDOW, :] = jnp.zeros((WINDOW, 2 * D_KV), _BF16)
        lc_scr[0:SUBLANES, :] = jnp.zeros((SUBLANES, D_LRU), _F32)
        sc_scr[0:SUBLANES, :] = jnp.zeros((SUBLANES, D_SC), _F32)
        h_scr[...] = jnp.zeros_like(h_scr)
        fhist_scr[...] = jnp.zeros_like(fhist_scr)

    @pl.when(t > 0)
    def _():
        kv_scr[0:WINDOW, :] = kv_scr[tb:tb + WINDOW, :]

    x = x_ref[0]
    hb = _rms(x, g_mix_ref[...]).astype(_BF16)
    z = jnp.dot(hb, w_in_ref[...], preferred_element_type=_F32)

    q = (z[:, _Q0:_Q0 + D_ATTN] * (HEAD_DIM ** -0.5)).astype(_BF16)
    kv_scr[WINDOW:WINDOW + tb, :] = z[:, _K0:_K0 + 2 * D_KV].astype(_BF16)
    kv_out_ref[0] = z[tb - WINDOW:tb, _K0:_K0 + 2 * D_KV]

    qi = lax.broadcasted_iota(jnp.int32, (WINDOW, 2 * WINDOW), 0)
    kj = lax.broadcasted_iota(jnp.int32, (WINDOW, 2 * WINDOW), 1)
    band = (kj >= qi) & (kj <= qi + WINDOW)
    band_first = band & (kj >= WINDOW * (1 - jnp.minimum(t, 1)))

    a_blocks = []
    for j in range(tb // WINDOW):
        keys = kv_scr[j * WINDOW:(j + 2) * WINDOW, :]
        qj = q[j * WINDOW:(j + 1) * WINDOW, :]
        mask = band_first if j == 0 else band
        outs = []
        for h in range(N_HEADS):
            g = h // GQA_GROUP
            sink = sinks_ref[h]
            s = lax.dot_general(qj[:, h * HEAD_DIM:(h + 1) * HEAD_DIM],
                                keys[:, g * HEAD_DIM:(g + 1) * HEAD_DIM], _NT,
                                preferred_element_type=_F32)
            s = jnp.where(mask, s, -jnp.inf)
            m = jnp.maximum(jnp.max(s, axis=-1, keepdims=True), sink)
            p = jnp.exp(s - m)
            denom = jnp.sum(p, axis=-1, keepdims=True) + jnp.exp(sink - m)
            o = jnp.dot(p.astype(_BF16),
                        keys[:, D_KV + g * HEAD_DIM:D_KV + (g + 1) * HEAD_DIM],
                        preferred_element_type=_F32)
            outs.append(o / denom)
        a_blocks.append(jnp.concatenate(outs, axis=-1))
    a_out = jnp.concatenate(a_blocks, axis=0)

    lx = z[:, _LX0:_LX0 + D_LRU]
    lc_scr[SUBLANES:SUBLANES + tb, :] = lx
    lcw = lcw_ref[...]
    xc = lcb_ref[...] + lcw[LRU_CONV - 1:LRU_CONV] * lx
    for k in range(LRU_CONV - 1):
        off = SUBLANES - (LRU_CONV - 1) + k
        xc = xc + lcw[k:k + 1] * lc_scr[off:off + tb, :]
    tail = lc_scr[tb + SUBLANES - (LRU_CONV - 1):tb + SUBLANES, :]
    lbuf_out_ref[0] = tail
    lc_scr[SUBLANES - (LRU_CONV - 1):SUBLANES, :] = tail

    gates = jnp.dot(xc.astype(_BF16), w_gates_ref[...],
                    preferred_element_type=_F32) + b_gates_ref[...]
    a_dec, b_in = _lru_coeffs(xc, gates, lam_ref[...])
    carry = h_scr[0:1, :]
    h_blocks = []
    for j in range(tb // WINDOW):
        a_cum, b_cum = _scan_rows(a_dec[j * WINDOW:(j + 1) * WINDOW],
                                  b_in[j * WINDOW:(j + 1) * WINDOW])
        hj = a_cum * carry + b_cum
        carry = hj[WINDOW - 1:WINDOW, :]
        h_blocks.append(hj)
    hseq = jnp.concatenate(h_blocks, axis=0)
    h_scr[0:1, :] = carry
    h_out_ref[0] = carry
    b_out = hseq * jax.nn.gelu(z[:, _LG0:_LG0 + D_LRU])

    cu = z[:, _SC0:_SC0 + D_SC] * z[:, _SU0:_SU0 + D_SC]
    sc_scr[SUBLANES:SUBLANES + tb, :] = cu
    scw = scw_ref[...]
    cconv = scw[SC_CONV - 1:SC_CONV] * cu
    for k in range(SC_CONV - 1):
        off = SUBLANES - (SC_CONV - 1) + k
        cconv = cconv + scw[k:k + 1] * sc_scr[off:off + tb, :]
    tail = sc_scr[tb + SUBLANES - (SC_CONV - 1):tb + SUBLANES, :]
    sbuf_out_ref[0] = tail
    sc_scr[SUBLANES - (SC_CONV - 1):SUBLANES, :] = tail
    c_out = z[:, _SB0:_SB0 + D_SC] * cconv

    g_out = g_out_ref[...]
    merged = jnp.concatenate([
        _rms(a_out, g_out[:, :D_ATTN]),
        _rms(b_out, g_out[:, D_ATTN:D_ATTN + D_LRU]),
        _rms(c_out, g_out[:, D_ATTN + D_LRU:])], axis=-1).astype(_BF16)
    x1 = x + jnp.dot(merged, w_out_ref[...], preferred_element_type=_F32)

    hn_scr[...] = _rms(x1, g_ffn_ref[...]).astype(_BF16)
    acc_scr[...] = x1

    def ffn_chunk(c, _):
        up = jnp.dot(hn_scr[...], w_up_ref[c], preferred_element_type=_F32)
        up_scr[SUBLANES:SUBLANES + tb, :] = up
        up_scr[SUBLANES - (FFN_CONV - 1):SUBLANES, :] = (
            fhist_scr[c, SUBLANES - (FFN_CONV - 1):SUBLANES, :])
        fcw = fcw_ref[c]
        upc = fcw[FFN_CONV - 1:FFN_CONV] * up
        for k in range(FFN_CONV - 1):
            off = SUBLANES - (FFN_CONV - 1) + k
            upc = upc + fcw[k:k + 1] * up_scr[off:off + tb, :]
        tail = up_scr[tb + SUBLANES - (FFN_CONV - 1):tb + SUBLANES, :]
        fhist_scr[c, SUBLANES - (FFN_CONV - 1):SUBLANES, :] = tail
        fbuf_out_ref[0, c] = tail
        act = (jax.nn.silu(upc[:, FF_CHUNK:]) * upc[:, :FF_CHUNK]).astype(_BF16)
        acc_scr[...] += jnp.dot(act, w_down_ref[c], preferred_element_type=_F32)
        return 0

    lax.fori_loop(0, N_FF_CHUNKS, ffn_chunk, 0)
    x2 = acc_scr[...]
    y_ref[0] = _rms(x2, g_final_ref[...]) if final else x2


def _const_spec(shape):
    zeros = (0,) * len(shape)
    return pl.BlockSpec(shape, lambda b, t: zeros, pipeline_mode=pl.Buffered(1))


def _prompt_layer(x, sinks, g_mix, w_in, lcw, lcb, w_gates, b_gates, lam, scw, g_out, w_out,
                  g_ffn, w_up, fcw, w_down, g_final, *, final):
    batch, seq, _ = x.shape
    tb = TOKEN_BLOCK
    consts = (g_mix, w_in, lcw, lcb, w_gates, b_gates, lam, scw, g_out, w_out, g_ffn, w_up,
              fcw, w_down, g_final)
    in_specs = [pl.BlockSpec(memory_space=pltpu.SMEM),
                pl.BlockSpec((1, tb, D_MODEL), lambda b, t: (b, t, 0))]
    in_specs += [_const_spec(c.shape) for c in consts]
    out_shape = (
        jax.ShapeDtypeStruct((batch, seq, D_MODEL), _F32),
        jax.ShapeDtypeStruct((batch, WINDOW, 2 * D_KV), _F32),
        jax.ShapeDtypeStruct((batch, 1, D_LRU), _F32),
        jax.ShapeDtypeStruct((batch, LRU_CONV - 1, D_LRU), _F32),
        jax.ShapeDtypeStruct((batch, SC_CONV - 1, D_SC), _F32),
        jax.ShapeDtypeStruct((batch, N_FF_CHUNKS, FFN_CONV - 1, 2 * FF_CHUNK), _F32),
    )
    out_specs = (
        pl.BlockSpec((1, tb, D_MODEL), lambda b, t: (b, t, 0)),
        pl.BlockSpec((1, WINDOW, 2 * D_KV), lambda b, t: (b, 0, 0)),
        pl.BlockSpec((1, 1, D_LRU), lambda b, t: (b, 0, 0)),
        pl.BlockSpec((1, LRU_CONV - 1, D_LRU), lambda b, t: (b, 0, 0)),
        pl.BlockSpec((1, SC_CONV - 1, D_SC), lambda b, t: (b, 0, 0)),
        pl.BlockSpec((1, N_FF_CHUNKS, FFN_CONV - 1, 2 * FF_CHUNK), lambda b, t: (b, 0, 0, 0)),
    )
    scratch = [
        pltpu.VMEM((WINDOW + tb, 2 * D_KV), _BF16),
        pltpu.VMEM((SUBLANES + tb, D_LRU), _F32),
        pltpu.VMEM((SUBLANES + tb, D_SC), _F32),
        pltpu.VMEM((SUBLANES, D_LRU), _F32),
        pltpu.VMEM((SUBLANES + tb, 2 * FF_CHUNK), _F32),
        pltpu.VMEM((N_FF_CHUNKS, SUBLANES, 2 * FF_CHUNK), _F32),
        pltpu.VMEM((tb, D_MODEL), _F32),
        pltpu.VMEM((tb, D_MODEL), _BF16),
    ]
    return pl.pallas_call(
        functools.partial(_prompt_layer_kernel, final=final),
        grid=(batch, seq // tb),
        in_specs=in_specs,
        out_specs=out_specs,
        out_shape=out_shape,
        scratch_shapes=scratch,
        compiler_params=pltpu.CompilerParams(
            dimension_semantics=("arbitrary", "arbitrary"),
            vmem_limit_bytes=VMEM_LIMIT_BYTES),
        name="prompt_layer",
    )(sinks, x, *consts)


def _sample_mix_kernel(
        x_ref, ck_ref, cv_ref, h0_ref, lbuf_ref, sbuf_ref, sink_rows_ref,
        g_mix_ref, w_q_ref, w_rest_ref, lcw_ref, lcb_ref, w_gates_ref, b_gates_ref, lam_ref,
        scw_ref, g_out_ref, w_out_ref,
        x1_ref, ck_out_ref, cv_out_ref, h_out_ref, lbuf_out_ref, sbuf_out_ref,
        q_scr, rest_scr, o_scr):
    r = pl.program_id(0)
    n_seq = x_ref.shape[0]
    gs = SEQ_GROUP
    lanes = D_KV

    @pl.when(r == 0)
    def _():
        hb = _rms(x_ref[...], g_mix_ref[...]).astype(_BF16)
        q_scr[...] = jnp.dot(hb, w_q_ref[...], preferred_element_type=_F32) * (HEAD_DIM ** -0.5)
        rest_scr[...] = jnp.dot(hb, w_rest_ref[...], preferred_element_type=_F32)

    row0 = pl.multiple_of(r * gs, gs)
    q_all = jnp.concatenate(
        [q_scr[pl.ds(row0, gs), h * lanes:(h + 1) * lanes] for h in range(N_HEADS)], axis=0)
    k_new = rest_scr[pl.ds(row0, gs), 0:D_KV]
    v_new = rest_scr[pl.ds(row0, gs), D_KV:2 * D_KV]
    k_new_rows = jnp.concatenate([k_new] * N_HEADS, axis=0)
    v_new_rows = jnp.concatenate([v_new] * N_HEADS, axis=0)

    keys = ck_ref[...].reshape(gs * WINDOW, lanes).astype(_BF16)
    s_all = lax.dot_general(q_all.astype(_BF16), keys, _NT,
                            preferred_element_type=_F32)
    seq_of_row = lax.broadcasted_iota(jnp.int32, (N_HEADS * gs, WINDOW), 0) & (gs - 1)
    s = jnp.zeros((N_HEADS * gs, WINDOW), _F32)
    for j in range(gs):
        s = s + jnp.where(seq_of_row == j, s_all[:, j * WINDOW:(j + 1) * WINDOW], 0.0)
    s_new = jnp.sum(q_all * k_new_rows, axis=-1, keepdims=True)
    sink = sink_rows_ref[:, 0:1]
    m = jnp.maximum(jnp.maximum(jnp.max(s, axis=-1, keepdims=True), s_new), sink)
    p = jnp.exp(s - m)
    p_new = jnp.exp(s_new - m)
    denom = jnp.sum(p, axis=-1, keepdims=True) + p_new + jnp.exp(sink - m)
    p_blk = jnp.concatenate(
        [jnp.where(seq_of_row == j, p, 0.0) for j in range(gs)], axis=-1).astype(_BF16)
    vals = cv_ref[...].reshape(gs * WINDOW, lanes).astype(_BF16)
    o = jnp.dot(p_blk, vals, preferred_element_type=_F32) + p_new * v_new_rows
    o = o / denom
    row = lax.broadcasted_iota(jnp.int32, (N_HEADS * gs, lanes), 0)
    lane = lax.broadcasted_iota(jnp.int32, (N_HEADS * gs, lanes), 1)
    own_kv = (lax.shift_right_logical(row, (GQA_GROUP * gs).bit_length() - 1)
              == lax.shift_right_logical(lane, HEAD_DIM.bit_length() - 1))
    o = jnp.where(own_kv, o, 0.0)
    for h in range(N_HEADS):
        o_scr[pl.ds(row0, gs), h * lanes:(h + 1) * lanes] = o[h * gs:(h + 1) * gs, :]

    ck_out_ref[:, 0:WINDOW - 1, :] = ck_ref[:, 1:WINDOW, :]
    cv_out_ref[:, 0:WINDOW - 1, :] = cv_ref[:, 1:WINDOW, :]
    for b in range(gs):
        ck_out_ref[b, WINDOW - 1:WINDOW, :] = k_new[b:b + 1, :]
        cv_out_ref[b, WINDOW - 1:WINDOW, :] = v_new[b:b + 1, :]

    @pl.when(r == n_seq // gs - 1)
    def _():
        rest = rest_scr[...]
        base = 2 * D_KV
        lx = rest[:, base:base + D_LRU]
        lg = rest[:, base + D_LRU:base + 2 * D_LRU]
        sb = rest[:, base + 2 * D_LRU:base + 2 * D_LRU + D_SC]
        sc = rest[:, base + 2 * D_LRU + D_SC:base + 2 * D_LRU + 2 * D_SC]
        su = rest[:, base + 2 * D_LRU + 2 * D_SC:]
        lbuf = lbuf_ref[...]
        lcw = lcw_ref[...]
        xc = lcb_ref[...] + lcw[LRU_CONV - 1:LRU_CONV] * lx
        for k in range(LRU_CONV - 1):
            xc = xc + lcw[k:k + 1] * lbuf[:, k * D_LRU:(k + 1) * D_LRU]
        lbuf_out_ref[...] = jnp.concatenate([lbuf[:, D_LRU:], lx], axis=-1)
        gates = jnp.dot(xc.astype(_BF16), w_gates_ref[...],
                        preferred_element_type=_F32) + b_gates_ref[...]
        a_dec, b_in = _lru_coeffs(xc, gates, lam_ref[...])
        h_new = a_dec * h0_ref[...] + b_in
        h_out_ref[...] = h_new
        b_out = h_new * jax.nn.gelu(lg)
        sbuf = sbuf_ref[...]
        cu = sc * su
        scw = scw_ref[...]
        cconv = scw[SC_CONV - 1:SC_CONV] * cu
        for k in range(SC_CONV - 1):
            cconv = cconv + scw[k:k + 1] * sbuf[:, k * D_SC:(k + 1) * D_SC]
        sbuf_out_ref[...] = jnp.concatenate([sbuf[:, D_SC:], cu], axis=-1)
        c_out = sb * cconv
        g_out = g_out_ref[...]
        n_a = N_HEADS * lanes
        o_pad = o_scr[...]
        a_scale = lax.rsqrt(jnp.sum(o_pad * o_pad, axis=-1, keepdims=True) / D_ATTN + EPS)
        merged = jnp.concatenate([
            o_pad * a_scale * g_out[:, :n_a],
            _rms(b_out, g_out[:, n_a:n_a + D_LRU]),
            _rms(c_out, g_out[:, n_a + D_LRU:])], axis=-1).astype(_BF16)
        x1_ref[...] = x_ref[...] + jnp.dot(merged, w_out_ref[...], preferred_element_type=_F32)


def _sample_mix(x, ck, cv, h0, lbuf, sbuf, sink_rows, g_mix, w_q, w_rest, lcw, lcb, w_gates,
                b_gates, lam, scw, g_out_pad, w_out_pad):
    n_seq = x.shape[0]
    gs = SEQ_GROUP
    lanes = D_KV

    def full(a):
        zeros = (0,) * a.ndim
        return pl.BlockSpec(a.shape, lambda r: zeros, pipeline_mode=pl.Buffered(1))

    cache_spec = pl.BlockSpec((gs, WINDOW, lanes), lambda r: (r, 0, 0))
    whole = (h0, lbuf, sbuf, sink_rows, g_mix, w_q, w_rest, lcw, lcb, w_gates, b_gates, lam, scw,
             g_out_pad, w_out_pad)
    out_shape = (
        jax.ShapeDtypeStruct((n_seq, D_MODEL), _F32),
        jax.ShapeDtypeStruct(ck.shape, _F32),
        jax.ShapeDtypeStruct(cv.shape, _F32),
        jax.ShapeDtypeStruct(h0.shape, _F32),
        jax.ShapeDtypeStruct(lbuf.shape, _F32),
        jax.ShapeDtypeStruct(sbuf.shape, _F32),
    )

    def full_out(s):
        zeros = (0,) * len(s.shape)
        return pl.BlockSpec(s.shape, lambda r: zeros)

    out_specs = (full_out(out_shape[0]), cache_spec, cache_spec, full_out(out_shape[3]),
                 full_out(out_shape[4]), full_out(out_shape[5]))
    scratch = [
        pltpu.VMEM((n_seq, N_HEADS * lanes), _F32),
        pltpu.VMEM((n_seq, D_IN - D_ATTN), _F32),
        pltpu.VMEM((n_seq, N_HEADS * lanes), _F32),
    ]
    return pl.pallas_call(
        _sample_mix_kernel,
        grid=(n_seq // gs,),
        in_specs=[full(x), cache_spec, cache_spec] + [full(a) for a in whole],
        out_specs=out_specs,
        out_shape=out_shape,
        scratch_shapes=scratch,
        compiler_params=pltpu.CompilerParams(
            dimension_semantics=("arbitrary",),
            vmem_limit_bytes=VMEM_LIMIT_BYTES),
        name="sample_mix",
    )(x, ck, cv, *whole)


def _sample_ffn_kernel(x1_ref, g_ffn_ref, w_up_ref, fcw_ref, w_down_ref,
                       u0_ref, u1_ref, g0_ref, g1_ref, g_final_ref,
                       y_ref, upu_ref, upg_ref, hn_scr, acc_scr, *, final):
    c = pl.program_id(0)

    @pl.when(c == 0)
    def _():
        x1 = x1_ref[...]
        hn_scr[...] = _rms(x1, g_ffn_ref[...]).astype(_BF16)
        acc_scr[...] = x1

    up = jnp.dot(hn_scr[...], w_up_ref[0], preferred_element_type=_F32)
    upu_ref[...] = up[:, :FF_CHUNK]
    upg_ref[...] = up[:, FF_CHUNK:]
    fcw = fcw_ref[0]
    older = jnp.concatenate([u0_ref[...], g0_ref[...]], axis=-1)
    old = jnp.concatenate([u1_ref[...], g1_ref[...]], axis=-1)
    upc = fcw[0:1] * older + fcw[1:2] * old + fcw[2:3] * up
    act = (jax.nn.silu(upc[:, FF_CHUNK:]) * upc[:, :FF_CHUNK]).astype(_BF16)
    acc_scr[...] += jnp.dot(act, w_down_ref[0], preferred_element_type=_F32)

    @pl.when(c == N_FF_CHUNKS - 1)
    def _():
        x2 = acc_scr[...]
        y_ref[...] = _rms(x2, g_final_ref[...]) if final else x2


def _sample_ffn(x1, g_ffn, w_up, fcw, w_down, fstate, g_final, *, final):
    n_seq = x1.shape[0]
    nc = N_FF_CHUNKS

    def full(a):
        zeros = (0,) * a.ndim
        return pl.BlockSpec(a.shape, lambda c: zeros, pipeline_mode=pl.Buffered(1))

    def state_spec(row, half):
        return pl.BlockSpec((n_seq, FF_CHUNK), lambda c: (0, (2 * row + half) * nc + c))

    in_specs = [
        full(x1), full(g_ffn),
        pl.BlockSpec((1, D_MODEL, 2 * FF_CHUNK), lambda c: (c, 0, 0)),
        pl.BlockSpec((1, FFN_CONV, 2 * FF_CHUNK), lambda c: (c, 0, 0)),
        pl.BlockSpec((1, FF_CHUNK, D_MODEL), lambda c: (c, 0, 0)),
        state_spec(0, 0), state_spec(1, 0), state_spec(0, 1), state_spec(1, 1),
        full(g_final),
    ]
    out_shape = (
        jax.ShapeDtypeStruct((n_seq, D_MODEL), _F32),
        jax.ShapeDtypeStruct((n_seq, D_FF), _F32),
        jax.ShapeDtypeStruct((n_seq, D_FF), _F32),
    )
    out_specs = (
        pl.BlockSpec((n_seq, D_MODEL), lambda c: (0, 0)),
        pl.BlockSpec((n_seq, FF_CHUNK), lambda c: (0, c)),
        pl.BlockSpec((n_seq, FF_CHUNK), lambda c: (0, c)),
    )
    return pl.pallas_call(
        functools.partial(_sample_ffn_kernel, final=final),
        grid=(nc,),
        in_specs=in_specs,
        out_specs=out_specs,
        out_shape=out_shape,
        scratch_shapes=[pltpu.VMEM((n_seq, D_MODEL), _BF16), pltpu.VMEM((n_seq, D_MODEL), _F32)],
        compiler_params=pltpu.CompilerParams(
            dimension_semantics=("arbitrary",),
            vmem_limit_bytes=VMEM_LIMIT_BYTES),
        name="sample_ffn",
    )(x1, g_ffn, w_up, fcw, w_down, fstate, fstate, fstate, fstate, g_final)


def _block_diag(w):
    eye = jnp.eye(N_LRU_BLOCKS, dtype=w.dtype)
    return jnp.einsum('lncd,nm->lncmd', w, eye).reshape(DEPTH, D_LRU, D_LRU)


def _chunk_cols(w):
    lead = w.shape[:-1]
    w = w.reshape(lead + (2, N_FF_CHUNKS, FF_CHUNK))
    w = jnp.moveaxis(w, -3, -2)
    return w.reshape(lead + (N_FF_CHUNKS, 2 * FF_CHUNK))


def _pad_heads(w, axis):
    w = jnp.moveaxis(w, axis, -1)
    lead = w.shape[:-1]
    w = w.reshape(lead + (N_KV_HEADS, GQA_GROUP, 1, HEAD_DIM))
    eye = jnp.eye(N_KV_HEADS, dtype=w.dtype).reshape(N_KV_HEADS, 1, N_KV_HEADS, 1)
    w = (w * eye).reshape(lead + (N_HEADS * N_KV_HEADS * HEAD_DIM,))
    return jnp.moveaxis(w, -1, axis)


def kernel(x_prompt, x_sample, cache_k, cache_v, state_rglru, state_lru_conv, state_sconv, state_ffn_conv, g_mix, w_in, sinks, lru_conv_w, lru_conv_b, lru_wa, lru_ba, lru_wi, lru_bi, lru_lambda, sc_conv_w, g_out, w_out, g_ffn, w_up, ffn_conv_w, w_down, g_final):
    batch = x_prompt.shape[0]
    n_seq = x_sample.shape[0]
    lanes = D_KV

    w_in_b = w_in.astype(_BF16)
    w_out_b = w_out.astype(_BF16)
    w_gates = jnp.concatenate([_block_diag(lru_wa), _block_diag(lru_wi)], axis=-1).astype(_BF16)
    b_gates = jnp.concatenate([lru_ba, lru_bi], axis=-1)[:, None, :]
    w_up_c = jnp.moveaxis(_chunk_cols(w_up.astype(_BF16)), 1, 2)
    fcw_c = jnp.moveaxis(_chunk_cols(ffn_conv_w), 1, 2)
    w_down_c = w_down.astype(_BF16).reshape(DEPTH, N_FF_CHUNKS, FF_CHUNK, D_MODEL)
    w_q_pad = _pad_heads(w_in[:, :, :D_ATTN], 2).astype(_BF16)
    w_rest_b = w_in_b[:, :, D_ATTN:]
    w_out_pad = jnp.concatenate([_pad_heads(w_out[:, :D_ATTN], 1), w_out[:, D_ATTN:]],
                                axis=1).astype(_BF16)
    g_out_pad = jnp.concatenate([_pad_heads(g_out[:, :D_ATTN], 1), g_out[:, D_ATTN:]], axis=1)
    sink_rows = jnp.broadcast_to(
        jnp.repeat(sinks, SEQ_GROUP, axis=1)[:, :, None], (DEPTH, N_HEADS * SEQ_GROUP, lanes))

    row = lambda a: a[:, None, :]
    g_mix_r, g_out_r, g_ffn_r = row(g_mix), row(g_out), row(g_ffn)
    lcb_r, lam_r, g_out_pad_r = row(lru_conv_b), row(lru_lambda), row(g_out_pad)
    g_final_r = g_final[None, :]

    ck = cache_k.reshape(DEPTH, n_seq, WINDOW, lanes)
    cv = cache_v.reshape(DEPTH, n_seq, WINDOW, lanes)
    lbuf = state_lru_conv.reshape(DEPTH, n_seq, (LRU_CONV - 1) * D_LRU)
    sbuf = state_sconv.reshape(DEPTH, n_seq, (SC_CONV - 1) * D_SC)
    fstate = state_ffn_conv.reshape(DEPTH, n_seq, (FFN_CONV - 1) * 2 * D_FF)

    xp = x_prompt
    xs = x_sample.reshape(n_seq, D_MODEL)
    p_states, s_states = [], []
    for l in range(DEPTH):
        final = l == DEPTH - 1
        xp, kv_new, h_last, lbuf_new, sbuf_new, fbuf_new = _prompt_layer(
            xp, sinks[l], g_mix_r[l], w_in_b[l], lru_conv_w[l], lcb_r[l], w_gates[l], b_gates[l],
            lam_r[l], sc_conv_w[l], g_out_r[l], w_out_b[l], g_ffn_r[l], w_up_c[l], fcw_c[l],
            w_down_c[l], g_final_r, final=final)
        fbuf_new = fbuf_new.reshape(batch, N_FF_CHUNKS, FFN_CONV - 1, 2, FF_CHUNK)
        fbuf_new = jnp.transpose(fbuf_new, (0, 2, 3, 1, 4)).reshape(batch, FFN_CONV - 1, 2 * D_FF)
        p_states.append((
            kv_new[:, :, :D_KV].reshape(batch, WINDOW, N_KV_HEADS, HEAD_DIM),
            kv_new[:, :, D_KV:].reshape(batch, WINDOW, N_KV_HEADS, HEAD_DIM),
            h_last.reshape(batch, D_LRU), lbuf_new, sbuf_new, fbuf_new))

        x1, ck_new, cv_new, h_new, lbuf_s, sbuf_s = _sample_mix(
            xs, ck[l], cv[l], state_rglru[l], lbuf[l], sbuf[l], sink_rows[l], g_mix_r[l],
            w_q_pad[l], w_rest_b[l], lru_conv_w[l], lcb_r[l], w_gates[l], b_gates[l], lam_r[l],
            sc_conv_w[l], g_out_pad_r[l], w_out_pad[l])
        xs, up_u, up_g = _sample_ffn(x1, g_ffn_r[l], w_up_c[l], fcw_c[l], w_down_c[l], fstate[l],
                                     g_final_r, final=final)
        f_new = jnp.stack([state_ffn_conv[l, :, FFN_CONV - 2],
                           jnp.concatenate([up_u, up_g], axis=-1)], axis=1)
        s_states.append((
            ck_new.reshape(n_seq, WINDOW, N_KV_HEADS, HEAD_DIM),
            cv_new.reshape(n_seq, WINDOW, N_KV_HEADS, HEAD_DIM),
            h_new,
            lbuf_s.reshape(n_seq, LRU_CONV - 1, D_LRU),
            sbuf_s.reshape(n_seq, SC_CONV - 1, D_SC),
            f_new))

    stack = lambda states, i: jnp.stack([s[i] for s in states])
    return ((xp, xs.reshape(n_seq, 1, D_MODEL))
            + tuple(stack(p_states, i) for i in range(6))
            + tuple(stack(s_states, i) for i in range(6)))
```

```python
import functools

import jax
import jax.numpy as jnp
from jax import lax
from jax.experimental import pallas as pl
from jax.experimental.pallas import tpu as pltpu

D_MODEL = 1024
DEPTH = 4
HEAD_DIM = 64
N_HEADS = 8
N_KV_HEADS = 2
GQA_GROUP = N_HEADS // N_KV_HEADS
WINDOW = 128
D_ATTN = N_HEADS * HEAD_DIM
D_KV = N_KV_HEADS * HEAD_DIM
D_LRU = 256
N_LRU_BLOCKS = 4
LRU_BLOCK = D_LRU // N_LRU_BLOCKS
LRU_CONV = 4
LRU_C = 8.0
D_SC = 256
SC_CONV = 3
D_MIX = D_ATTN + D_LRU + D_SC
D_IN = D_ATTN + 2 * D_KV + 2 * D_LRU + 3 * D_SC
D_FF = 2816
FFN_CONV = 3
EPS = 1e-6

_Q0, _K0, _V0 = 0, D_ATTN, D_ATTN + D_KV
_LX0 = D_ATTN + 2 * D_KV
_LG0 = _LX0 + D_LRU
_SB0 = _LG0 + D_LRU
_SC0 = _SB0 + D_SC
_SU0 = _SC0 + D_SC

SUBLANES = 8
LANES = 128
TOKEN_BLOCK = 512
FF_CHUNK = 256
N_FF_CHUNKS = D_FF // FF_CHUNK
SEQ_GROUP = 8
VMEM_LIMIT_BYTES = 56 * 1024 * 1024

_F32 = jnp.float32
_BF16 = jnp.bfloat16
_NT = (((1,), (1,)), ((), ()))


def _rms(x, g):
    ms = jnp.mean(x * x, axis=-1, keepdims=True)
    return x * lax.rsqrt(ms + EPS) * g


def _softplus(x):
    return jnp.maximum(x, 0.0) + jnp.log1p(jnp.exp(-jnp.abs(x)))


def _lru_coeffs(xc, gates, lam):
    r = jax.nn.sigmoid(gates[:, :D_LRU])
    i = jax.nn.sigmoid(gates[:, D_LRU:])
    log_a = (-LRU_C) * r * _softplus(-lam)
    a = jnp.exp(log_a)
    b = jnp.sqrt(jnp.tanh(-log_a) * (1.0 + a * a)) * (i * xc)
    return a, b


def _scan_rows(a, b):
    n = a.shape[0]
    rows = lax.broadcasted_iota(jnp.int32, a.shape, 0)
    d = 1
    while d < SUBLANES:
        keep = rows >= d
        a_prev = jnp.where(keep, pltpu.roll(a, d, 0), 1.0)
        b_prev = jnp.where(keep, pltpu.roll(b, d, 0), 0.0)
        b = a * b_prev + b
        a = a * a_prev
        d *= 2
    while d < n:
        b = jnp.concatenate([b[:d], a[d:] * b[:-d] + b[d:]], axis=0)
        a = jnp.concatenate([a[:d], a[d:] * a[:-d]], axis=0)
        d *= 2
    return a, b


def _causal_conv(u, w_ref, scr, hist=None):
    n, c = u.shape
    taps = w_ref.shape[0]
    h0 = SUBLANES - (taps - 1)
    outs, tails = [], []
    for j in range(c // LANES):
        cols = slice(j * LANES, (j + 1) * LANES)
        uj = u[:, cols]
        scr[j, SUBLANES:SUBLANES + n, :] = uj
        if hist is not None:
            scr[j, h0:SUBLANES, :] = hist[j, h0:SUBLANES, :]
        y = w_ref[taps - 1:taps, cols] * uj
        for k in range(taps - 1):
            y = y + w_ref[k:k + 1, cols] * scr[j, h0 + k:h0 + k + n, :]
        tail = scr[j, n + h0:n + SUBLANES, :]
        (scr if hist is None else hist)[j, h0:SUBLANES, :] = tail
        outs.append(y)
        tails.append(tail)
    return jnp.concatenate(outs, axis=-1), jnp.concatenate(tails, axis=-1)


def _prompt_layer_kernel(
        sinks_ref, x_ref, g_mix_ref, w_in_ref, lcw_ref, lcb_ref, w_gates_ref, b_gates_ref,
        lam_ref, scw_ref, g_out_ref, w_out_ref, g_ffn_ref, w_up_ref, fcw_ref, w_down_ref,
        g_final_ref,
        y_ref, kv_out_ref, h_out_ref, lbuf_out_ref, sbuf_out_ref, fbuf_out_ref,
        kv_scr, lc_scr, sc_scr, h_scr, up_scr, fhist_scr, x1_scr, hn_scr, act_scr,
        *, final):
    tb = TOKEN_BLOCK
    t = pl.program_id(1)

    @pl.when(t == 0)
    def _():
        kv_scr[0:WINDOW, :] = jnp.zeros((WINDOW, 2 * D_KV), _BF16)
        lc_scr[:, 0:SUBLANES, :] = jnp.zeros((D_LRU // LANES, SUBLANES, LANES), _F32)
        sc_scr[:, 0:SUBLANES, :] = jnp.zeros((D_SC // LANES, SUBLANES, LANES), _F32)
        h_scr[...] = jnp.zeros_like(h_scr)
        fhist_scr[...] = jnp.zeros_like(fhist_scr)

    @pl.when(t > 0)
    def _():
        kv_scr[0:WINDOW, :] = kv_scr[tb:tb + WINDOW, :]

    x = x_ref[0]
    hb = _rms(x, g_mix_ref[...]).astype(_BF16)
    z = jnp.dot(hb, w_in_ref[...], preferred_element_type=_F32)

    q = (z[:, _Q0:_Q0 + D_ATTN] * (HEAD_DIM ** -0.5)).astype(_BF16)
    kv_scr[WINDOW:WINDOW + tb, :] = z[:, _K0:_K0 + 2 * D_KV].astype(_BF16)
    kv_out_ref[0] = z[tb - WINDOW:tb, _K0:_K0 + 2 * D_KV]

    qi = lax.broadcasted_iota(jnp.int32, (WINDOW, 2 * WINDOW), 0)
    kj = lax.broadcasted_iota(jnp.int32, (WINDOW, 2 * WINDOW), 1)
    band = (kj >= qi) & (kj <= qi + WINDOW)
    band_first = band & (kj >= WINDOW * (1 - jnp.minimum(t, 1)))

    a_blocks = []
    for j in range(tb // WINDOW):
        keys = kv_scr[j * WINDOW:(j + 2) * WINDOW, :]
        qj = q[j * WINDOW:(j + 1) * WINDOW, :]
        mask = band_first if j == 0 else band
        outs = []
        for h in range(N_HEADS):
            g = h // GQA_GROUP
            sink = sinks_ref[h]
            s = lax.dot_general(qj[:, h * HEAD_DIM:(h + 1) * HEAD_DIM],
                                keys[:, g * HEAD_DIM:(g + 1) * HEAD_DIM], _NT,
                                preferred_element_type=_F32)
            s = jnp.where(mask, s, -jnp.inf)
            m = jnp.maximum(jnp.max(s, axis=-1, keepdims=True), sink)
            p = jnp.exp(s - m)
            denom = jnp.sum(p, axis=-1, keepdims=True) + jnp.exp(sink - m)
            o = jnp.dot(p.astype(_BF16),
                        keys[:, D_KV + g * HEAD_DIM:D_KV + (g + 1) * HEAD_DIM],
                        preferred_element_type=_F32)
            outs.append(o / denom)
        a_blocks.append(jnp.concatenate(outs, axis=-1))
    a_out = jnp.concatenate(a_blocks, axis=0)

    xc, tail = _causal_conv(z[:, _LX0:_LX0 + D_LRU], lcw_ref, lc_scr)
    xc = xc + lcb_ref[...]
    lbuf_out_ref[0] = tail

    gates = jnp.dot(xc.astype(_BF16), w_gates_ref[...],
                    preferred_element_type=_F32) + b_gates_ref[...]
    a_dec, b_in = _lru_coeffs(xc, gates, lam_ref[...])
    carry = h_scr[0:1, :]
    h_blocks = []
    for j in range(tb // WINDOW):
        a_cum, b_cum = _scan_rows(a_dec[j * WINDOW:(j + 1) * WINDOW],
                                  b_in[j * WINDOW:(j + 1) * WINDOW])
        hj = a_cum * carry + b_cum
        carry = hj[WINDOW - 1:WINDOW, :]
        h_blocks.append(hj)
    hseq = jnp.concatenate(h_blocks, axis=0)
    h_scr[0:1, :] = carry
    h_out_ref[0] = carry
    b_out = hseq * jax.nn.gelu(z[:, _LG0:_LG0 + D_LRU])

    cu = z[:, _SC0:_SC0 + D_SC] * z[:, _SU0:_SU0 + D_SC]
    cconv, tail = _causal_conv(cu, scw_ref, sc_scr)
    sbuf_out_ref[0] = tail
    c_out = z[:, _SB0:_SB0 + D_SC] * cconv

    g_out = g_out_ref[...]
    merged = jnp.concatenate([
        _rms(a_out, g_out[:, :D_ATTN]),
        _rms(b_out, g_out[:, D_ATTN:D_ATTN + D_LRU]),
        _rms(c_out, g_out[:, D_ATTN + D_LRU:])], axis=-1).astype(_BF16)
    x1 = x + jnp.dot(merged, w_out_ref[...], preferred_element_type=_F32)

    hn_scr[...] = _rms(x1, g_ffn_ref[...]).astype(_BF16)
    x1_scr[...] = x1
    for c in range(N_FF_CHUNKS):
        up = jnp.dot(hn_scr[...], w_up_ref[c], preferred_element_type=_F32)
        upc, tail = _causal_conv(up, fcw_ref.at[c], up_scr.at[c % 2], hist=fhist_scr.at[c])
        fbuf_out_ref[0, c] = tail
        act = jax.nn.silu(upc[:, FF_CHUNK:]) * upc[:, :FF_CHUNK]
        act_scr[:, c * FF_CHUNK:(c + 1) * FF_CHUNK] = act.astype(_BF16)
    x2 = x1_scr[...] + jnp.dot(act_scr[...], w_down_ref[...], preferred_element_type=_F32)
    y_ref[0] = _rms(x2, g_final_ref[...]) if final else x2


def _const_spec(shape):
    zeros = (0,) * len(shape)
    return pl.BlockSpec(shape, lambda b, t: zeros, pipeline_mode=pl.Buffered(1))


def _prompt_layer(x, sinks, g_mix, w_in, lcw, lcb, w_gates, b_gates, lam, scw, g_out, w_out,
                  g_ffn, w_up, fcw, w_down, g_final, *, final):
    batch, seq, _ = x.shape
    tb = TOKEN_BLOCK
    consts = (g_mix, w_in, lcw, lcb, w_gates, b_gates, lam, scw, g_out, w_out, g_ffn, w_up,
              fcw, w_down, g_final)
    in_specs = [pl.BlockSpec(memory_space=pltpu.SMEM),
                pl.BlockSpec((1, tb, D_MODEL), lambda b, t: (b, t, 0))]
    in_specs += [_const_spec(c.shape) for c in consts]
    out_shape = (
        jax.ShapeDtypeStruct((batch, seq, D_MODEL), _F32),
        jax.ShapeDtypeStruct((batch, WINDOW, 2 * D_KV), _F32),
        jax.ShapeDtypeStruct((batch, 1, D_LRU), _F32),
        jax.ShapeDtypeStruct((batch, LRU_CONV - 1, D_LRU), _F32),
        jax.ShapeDtypeStruct((batch, SC_CONV - 1, D_SC), _F32),
        jax.ShapeDtypeStruct((batch, N_FF_CHUNKS, FFN_CONV - 1, 2 * FF_CHUNK), _F32),
    )
    out_specs = (
        pl.BlockSpec((1, tb, D_MODEL), lambda b, t: (b, t, 0)),
        pl.BlockSpec((1, WINDOW, 2 * D_KV), lambda b, t: (b, 0, 0)),
        pl.BlockSpec((1, 1, D_LRU), lambda b, t: (b, 0, 0)),
        pl.BlockSpec((1, LRU_CONV - 1, D_LRU), lambda b, t: (b, 0, 0)),
        pl.BlockSpec((1, SC_CONV - 1, D_SC), lambda b, t: (b, 0, 0)),
        pl.BlockSpec((1, N_FF_CHUNKS, FFN_CONV - 1, 2 * FF_CHUNK), lambda b, t: (b, 0, 0, 0)),
    )
    scratch = [
        pltpu.VMEM((WINDOW + tb, 2 * D_KV), _BF16),
        pltpu.VMEM((D_LRU // LANES, SUBLANES + tb, LANES), _F32),
        pltpu.VMEM((D_SC // LANES, SUBLANES + tb, LANES), _F32),
        pltpu.VMEM((SUBLANES, D_LRU), _F32),
        pltpu.VMEM((2, 2 * FF_CHUNK // LANES, SUBLANES + tb, LANES), _F32),
        pltpu.VMEM((N_FF_CHUNKS, 2 * FF_CHUNK // LANES, SUBLANES, LANES), _F32),
        pltpu.VMEM((tb, D_MODEL), _F32),
        pltpu.VMEM((tb, D_MODEL), _BF16),
        pltpu.VMEM((tb, D_FF), _BF16),
    ]
    return pl.pallas_call(
        functools.partial(_prompt_layer_kernel, final=final),
        grid=(batch, seq // tb),
        in_specs=in_specs,
        out_specs=out_specs,
        out_shape=out_shape,
        scratch_shapes=scratch,
        compiler_params=pltpu.CompilerParams(
            dimension_semantics=("arbitrary", "arbitrary"),
            vmem_limit_bytes=VMEM_LIMIT_BYTES),
        name="prompt_layer",
    )(sinks, x, *consts)


def _sample_mix_kernel(
        x_ref, ck_ref, cv_ref, h0_ref, lbuf_ref, sbuf_ref, sink_rows_ref,
        g_mix_ref, w_q_ref, w_rest_ref, lcw_ref, lcb_ref, w_gates_ref, b_gates_ref, lam_ref,
        scw_ref, g_out_ref, w_out_ref,
        x1_ref, ck_out_ref, cv_out_ref, h_out_ref, lbuf_out_ref, sbuf_out_ref,
        q_scr, rest_scr, o_scr):
    r = pl.program_id(0)
    n_seq = x_ref.shape[0]
    gs = SEQ_GROUP
    lanes = D_KV

    @pl.when(r == 0)
    def _():
        hb = _rms(x_ref[...], g_mix_ref[...]).astype(_BF16)
        q_scr[...] = jnp.dot(hb, w_q_ref[...], preferred_element_type=_F32) * (HEAD_DIM ** -0.5)
        rest_scr[...] = jnp.dot(hb, w_rest_ref[...], preferred_element_type=_F32)

    row0 = pl.multiple_of(r * gs, gs)
    q_all = jnp.concatenate(
        [q_scr[pl.ds(row0, gs), h * lanes:(h + 1) * lanes] for h in range(N_HEADS)], axis=0)
    k_new = rest_scr[pl.ds(row0, gs), 0:D_KV]
    v_new = rest_scr[pl.ds(row0, gs), D_KV:2 * D_KV]
    k_new_rows = jnp.concatenate([k_new] * N_HEADS, axis=0)
    v_new_rows = jnp.concatenate([v_new] * N_HEADS, axis=0)

    keys = ck_ref[...].reshape(gs * WINDOW, lanes).astype(_BF16)
    s_all = lax.dot_general(q_all.astype(_BF16), keys, _NT,
                            preferred_element_type=_F32)
    seq_of_row = lax.broadcasted_iota(jnp.int32, (N_HEADS * gs, WINDOW), 0) & (gs - 1)
    s = jnp.zeros((N_HEADS * gs, WINDOW), _F32)
    for j in range(gs):
        s = s + jnp.where(seq_of_row == j, s_all[:, j * WINDOW:(j + 1) * WINDOW], 0.0)
    s_new = jnp.sum(q_all * k_new_rows, axis=-1, keepdims=True)
    sink = sink_rows_ref[:, 0:1]
    m = jnp.maximum(jnp.maximum(jnp.max(s, axis=-1, keepdims=True), s_new), sink)
    p = jnp.exp(s - m)
    p_new = jnp.exp(s_new - m)
    denom = jnp.sum(p, axis=-1, keepdims=True) + p_new + jnp.exp(sink - m)
    p_blk = jnp.concatenate(
        [jnp.where(seq_of_row == j, p, 0.0) for j in range(gs)], axis=-1).astype(_BF16)
    vals = cv_ref[...].reshape(gs * WINDOW, lanes).astype(_BF16)
    o = jnp.dot(p_blk, vals, preferred_element_type=_F32) + p_new * v_new_rows
    o = o / denom
    row = lax.broadcasted_iota(jnp.int32, (N_HEADS * gs, lanes), 0)
    lane = lax.broadcasted_iota(jnp.int32, (N_HEADS * gs, lanes), 1)
    own_kv = (lax.shift_right_logical(row, (GQA_GROUP * gs).bit_length() - 1)
              == lax.shift_right_logical(lane, HEAD_DIM.bit_length() - 1))
    o = jnp.where(own_kv, o, 0.0)
    for h in range(N_HEADS):
        o_scr[pl.ds(row0, gs), h * lanes:(h + 1) * lanes] = o[h * gs:(h + 1) * gs, :]

    ck_out_ref[:, 0:WINDOW - 1, :] = ck_ref[:, 1:WINDOW, :]
    cv_out_ref[:, 0:WINDOW - 1, :] = cv_ref[:, 1:WINDOW, :]
    for b in range(gs):
        ck_out_ref[b, WINDOW - 1:WINDOW, :] = k_new[b:b + 1, :]
        cv_out_ref[b, WINDOW - 1:WINDOW, :] = v_new[b:b + 1, :]

    @pl.when(r == n_seq // gs - 1)
    def _():
        rest = rest_scr[...]
        base = 2 * D_KV
        lx = rest[:, base:base + D_LRU]
        lg = rest[:, base + D_LRU:base + 2 * D_LRU]
        sb = rest[:, base + 2 * D_LRU:base + 2 * D_LRU + D_SC]
        sc = rest[:, base + 2 * D_LRU + D_SC:base + 2 * D_LRU + 2 * D_SC]
        su = rest[:, base + 2 * D_LRU + 2 * D_SC:]
        lbuf = lbuf_ref[...]
        lcw = lcw_ref[...]
        xc = lcb_ref[...] + lcw[LRU_CONV - 1:LRU_CONV] * lx
        for k in range(LRU_CONV - 1):
            xc = xc + lcw[k:k + 1] * lbuf[:, k * D_LRU:(k + 1) * D_LRU]
        lbuf_out_ref[...] = jnp.concatenate([lbuf[:, D_LRU:], lx], axis=-1)
        gates = jnp.dot(xc.astype(_BF16), w_gates_ref[...],
                        preferred_element_type=_F32) + b_gates_ref[...]
        a_dec, b_in = _lru_coeffs(xc, gates, lam_ref[...])
        h_new = a_dec * h0_ref[...] + b_in
        h_out_ref[...] = h_new
        b_out = h_new * jax.nn.gelu(lg)
        sbuf = sbuf_ref[...]
        cu = sc * su
        scw = scw_ref[...]
        cconv = scw[SC_CONV - 1:SC_CONV] * cu
        for k in range(SC_CONV - 1):
            cconv = cconv + scw[k:k + 1] * sbuf[:, k * D_SC:(k + 1) * D_SC]
        sbuf_out_ref[...] = jnp.concatenate([sbuf[:, D_SC:], cu], axis=-1)
        c_out = sb * cconv
        g_out = g_out_ref[...]
        n_a = N_HEADS * lanes
        o_pad = o_scr[...]
        a_scale = lax.rsqrt(jnp.sum(o_pad * o_pad, axis=-1, keepdims=True) / D_ATTN + EPS)
        merged = jnp.concatenate([
            o_pad * a_scale * g_out[:, :n_a],
            _rms(b_out, g_out[:, n_a:n_a + D_LRU]),
            _rms(c_out, g_out[:, n_a + D_LRU:])], axis=-1).astype(_BF16)
        x1_ref[...] = x_ref[...] + jnp.dot(merged, w_out_ref[...], preferred_element_type=_F32)


def _sample_mix(x, ck, cv, h0, lbuf, sbuf, sink_rows, g_mix, w_q, w_rest, lcw, lcb, w_gates,
                b_gates, lam, scw, g_out_pad, w_out_pad):
    n_seq = x.shape[0]
    gs = SEQ_GROUP
    lanes = D_KV

    def full(a):
        zeros = (0,) * a.ndim
        return pl.BlockSpec(a.shape, lambda r: zeros, pipeline_mode=pl.Buffered(1))

    cache_spec = pl.BlockSpec((gs, WINDOW, lanes), lambda r: (r, 0, 0))
    whole = (h0, lbuf, sbuf, sink_rows, g_mix, w_q, w_rest, lcw, lcb, w_gates, b_gates, lam, scw,
             g_out_pad, w_out_pad)
    out_shape = (
        jax.ShapeDtypeStruct((n_seq, D_MODEL), _F32),
        jax.ShapeDtypeStruct(ck.shape, _F32),
        jax.ShapeDtypeStruct(cv.shape, _F32),
        jax.ShapeDtypeStruct(h0.shape, _F32),
        jax.ShapeDtypeStruct(lbuf.shape, _F32),
        jax.ShapeDtypeStruct(sbuf.shape, _F32),
    )

    def full_out(s):
        zeros = (0,) * len(s.shape)
        return pl.BlockSpec(s.shape, lambda r: zeros)

    out_specs = (full_out(out_shape[0]), cache_spec, cache_spec, full_out(out_shape[3]),
                 full_out(out_shape[4]), full_out(out_shape[5]))
    scratch = [
        pltpu.VMEM((n_seq, N_HEADS * lanes), _F32),
        pltpu.VMEM((n_seq, D_IN - D_ATTN), _F32),
        pltpu.VMEM((n_seq, N_HEADS * lanes), _F32),
    ]
    return pl.pallas_call(
        _sample_mix_kernel,
        grid=(n_seq // gs,),
        in_specs=[full(x), cache_spec, cache_spec] + [full(a) for a in whole],
        out_specs=out_specs,
        out_shape=out_shape,
        scratch_shapes=scratch,
        compiler_params=pltpu.CompilerParams(
            dimension_semantics=("arbitrary",),
            vmem_limit_bytes=VMEM_LIMIT_BYTES),
        name="sample_mix",
    )(x, ck, cv, *whole)


def _sample_ffn_kernel(x1_ref, g_ffn_ref, w_up_ref, fcw_ref, w_down_ref,
                       u0_ref, u1_ref, g0_ref, g1_ref, g_final_ref,
                       y_ref, upu_ref, upg_ref, hn_scr, acc_scr, *, final):
    c = pl.program_id(0)

    @pl.when(c == 0)
    def _():
        x1 = x1_ref[...]
        hn_scr[...] = _rms(x1, g_ffn_ref[...]).astype(_BF16)
        acc_scr[...] = x1

    up = jnp.dot(hn_scr[...], w_up_ref[0], preferred_element_type=_F32)
    upu_ref[...] = up[:, :FF_CHUNK]
    upg_ref[...] = up[:, FF_CHUNK:]
    fcw = fcw_ref[0]
    older = jnp.concatenate([u0_ref[...], g0_ref[...]], axis=-1)
    old = jnp.concatenate([u1_ref[...], g1_ref[...]], axis=-1)
    upc = fcw[0:1] * older + fcw[1:2] * old + fcw[2:3] * up
    act = (jax.nn.silu(upc[:, FF_CHUNK:]) * upc[:, :FF_CHUNK]).astype(_BF16)
    acc_scr[...] += jnp.dot(act, w_down_ref[...], preferred_element_type=_F32)

    @pl.when(c == N_FF_CHUNKS - 1)
    def _():
        x2 = acc_scr[...]
        y_ref[...] = _rms(x2, g_final_ref[...]) if final else x2


def _sample_ffn(x1, g_ffn, w_up, fcw, w_down, fstate, g_final, *, final):
    n_seq = x1.shape[0]
    nc = N_FF_CHUNKS

    def full(a):
        zeros = (0,) * a.ndim
        return pl.BlockSpec(a.shape, lambda c: zeros, pipeline_mode=pl.Buffered(1))

    def state_spec(row, half):
        return pl.BlockSpec((n_seq, FF_CHUNK), lambda c: (0, (2 * row + half) * nc + c))

    in_specs = [
        full(x1), full(g_ffn),
        pl.BlockSpec((1, D_MODEL, 2 * FF_CHUNK), lambda c: (c, 0, 0)),
        pl.BlockSpec((1, FFN_CONV, 2 * FF_CHUNK), lambda c: (c, 0, 0)),
        pl.BlockSpec((FF_CHUNK, D_MODEL), lambda c: (c, 0)),
        state_spec(0, 0), state_spec(1, 0), state_spec(0, 1), state_spec(1, 1),
        full(g_final),
    ]
    out_shape = (
        jax.ShapeDtypeStruct((n_seq, D_MODEL), _F32),
        jax.ShapeDtypeStruct((n_seq, D_FF), _F32),
        jax.ShapeDtypeStruct((n_seq, D_FF), _F32),
    )
    out_specs = (
        pl.BlockSpec((n_seq, D_MODEL), lambda c: (0, 0)),
        pl.BlockSpec((n_seq, FF_CHUNK), lambda c: (0, c)),
        pl.BlockSpec((n_seq, FF_CHUNK), lambda c: (0, c)),
    )
    return pl.pallas_call(
        functools.partial(_sample_ffn_kernel, final=final),
        grid=(nc,),
        in_specs=in_specs,
        out_specs=out_specs,
        out_shape=out_shape,
        scratch_shapes=[pltpu.VMEM((n_seq, D_MODEL), _BF16), pltpu.VMEM((n_seq, D_MODEL), _F32)],
        compiler_params=pltpu.CompilerParams(
            dimension_semantics=("arbitrary",),
            vmem_limit_bytes=VMEM_LIMIT_BYTES),
        name="sample_ffn",
    )(x1, g_ffn, w_up, fcw, w_down, fstate, fstate, fstate, fstate, g_final)


def _block_diag(w):
    eye = jnp.eye(N_LRU_BLOCKS, dtype=w.dtype)
    return jnp.einsum('lncd,nm->lncmd', w, eye).reshape(DEPTH, D_LRU, D_LRU)


def _chunk_cols(w):
    lead = w.shape[:-1]
    w = w.reshape(lead + (2, N_FF_CHUNKS, FF_CHUNK))
    w = jnp.moveaxis(w, -3, -2)
    return w.reshape(lead + (N_FF_CHUNKS, 2 * FF_CHUNK))


def _pad_heads(w, axis):
    w = jnp.moveaxis(w, axis, -1)
    lead = w.shape[:-1]
    w = w.reshape(lead + (N_KV_HEADS, GQA_GROUP, 1, HEAD_DIM))
    eye = jnp.eye(N_KV_HEADS, dtype=w.dtype).reshape(N_KV_HEADS, 1, N_KV_HEADS, 1)
    w = (w * eye).reshape(lead + (N_HEADS * N_KV_HEADS * HEAD_DIM,))
    return jnp.moveaxis(w, -1, axis)


def kernel(x_prompt, x_sample, cache_k, cache_v, state_rglru, state_lru_conv, state_sconv, state_ffn_conv, g_mix, w_in, sinks, lru_conv_w, lru_conv_b, lru_wa, lru_ba, lru_wi, lru_bi, lru_lambda, sc_conv_w, g_out, w_out, g_ffn, w_up, ffn_conv_w, w_down, g_final):
    batch = x_prompt.shape[0]
    n_seq = x_sample.shape[0]
    lanes = D_KV

    w_in_b = w_in.astype(_BF16)
    w_out_b = w_out.astype(_BF16)
    w_gates = jnp.concatenate([_block_diag(lru_wa), _block_diag(lru_wi)], axis=-1).astype(_BF16)
    b_gates = jnp.concatenate([lru_ba, lru_bi], axis=-1)[:, None, :]
    w_up_c = jnp.moveaxis(_chunk_cols(w_up.astype(_BF16)), 1, 2)
    fcw_c = jnp.moveaxis(_chunk_cols(ffn_conv_w), 1, 2)
    w_down_b = w_down.astype(_BF16)
    w_q_pad = _pad_heads(w_in[:, :, :D_ATTN], 2).astype(_BF16)
    w_rest_b = w_in_b[:, :, D_ATTN:]
    w_out_pad = jnp.concatenate([_pad_heads(w_out[:, :D_ATTN], 1), w_out[:, D_ATTN:]],
                                axis=1).astype(_BF16)
    g_out_pad = jnp.concatenate([_pad_heads(g_out[:, :D_ATTN], 1), g_out[:, D_ATTN:]], axis=1)
    sink_rows = jnp.broadcast_to(
        jnp.repeat(sinks, SEQ_GROUP, axis=1)[:, :, None], (DEPTH, N_HEADS * SEQ_GROUP, lanes))

    row = lambda a: a[:, None, :]
    g_mix_r, g_out_r, g_ffn_r = row(g_mix), row(g_out), row(g_ffn)
    lcb_r, lam_r, g_out_pad_r = row(lru_conv_b), row(lru_lambda), row(g_out_pad)
    g_final_r = g_final[None, :]

    ck = cache_k.reshape(DEPTH, n_seq, WINDOW, lanes)
    cv = cache_v.reshape(DEPTH, n_seq, WINDOW, lanes)
    lbuf = state_lru_conv.reshape(DEPTH, n_seq, (LRU_CONV - 1) * D_LRU)
    sbuf = state_sconv.reshape(DEPTH, n_seq, (SC_CONV - 1) * D_SC)
    fstate = state_ffn_conv.reshape(DEPTH, n_seq, (FFN_CONV - 1) * 2 * D_FF)

    xp = x_prompt
    xs = x_sample.reshape(n_seq, D_MODEL)
    p_states, s_states = [], []
    for l in range(DEPTH):
        final = l == DEPTH - 1
        xp, kv_new, h_last, lbuf_new, sbuf_new, fbuf_new = _prompt_layer(
            xp, sinks[l], g_mix_r[l], w_in_b[l], lru_conv_w[l], lcb_r[l], w_gates[l], b_gates[l],
            lam_r[l], sc_conv_w[l], g_out_r[l], w_out_b[l], g_ffn_r[l], w_up_c[l], fcw_c[l],
            w_down_b[l], g_final_r, final=final)
        fbuf_new = fbuf_new.reshape(batch, N_FF_CHUNKS, FFN_CONV - 1, 2, FF_CHUNK)
        fbuf_new = jnp.transpose(fbuf_new, (0, 2, 3, 1, 4)).reshape(batch, FFN_CONV - 1, 2 * D_FF)
        p_states.append((
            kv_new[:, :, :D_KV].reshape(batch, WINDOW, N_KV_HEADS, HEAD_DIM),
            kv_new[:, :, D_KV:].reshape(batch, WINDOW, N_KV_HEADS, HEAD_DIM),
            h_last.reshape(batch, D_LRU), lbuf_new, sbuf_new, fbuf_new))

        x1, ck_new, cv_new, h_new, lbuf_s, sbuf_s = _sample_mix(
            xs, ck[l], cv[l], state_rglru[l], lbuf[l], sbuf[l], sink_rows[l], g_mix_r[l],
            w_q_pad[l], w_rest_b[l], lru_conv_w[l], lcb_r[l], w_gates[l], b_gates[l], lam_r[l],
            sc_conv_w[l], g_out_pad_r[l], w_out_pad[l])
        xs, up_u, up_g = _sample_ffn(x1, g_ffn_r[l], w_up_c[l], fcw_c[l], w_down_b[l], fstate[l],
                                     g_final_r, final=final)
        f_new = jnp.stack([state_ffn_conv[l, :, FFN_CONV - 2],
                           jnp.concatenate([up_u, up_g], axis=-1)], axis=1)
        s_states.append((
            ck_new.reshape(n_seq, WINDOW, N_KV_HEADS, HEAD_DIM),
            cv_new.reshape(n_seq, WINDOW, N_KV_HEADS, HEAD_DIM),
            h_new,
            lbuf_s.reshape(n_seq, LRU_CONV - 1, D_LRU),
            sbuf_s.reshape(n_seq, SC_CONV - 1, D_SC),
            f_new))

    stack = lambda states, i: jnp.stack([s[i] for s in states])
    return ((xp, xs.reshape(n_seq, 1, D_MODEL))
            + tuple(stack(p_states, i) for i in range(6))
            + tuple(stack(s_states, i) for i in range(6)))
```

```python
import functools

import jax
import jax.numpy as jnp
from jax import lax
from jax.experimental import pallas as pl
from jax.experimental.pallas import tpu as pltpu

D_MODEL = 1024
DEPTH = 4
HEAD_DIM = 64
N_HEADS = 8
N_KV_HEADS = 2
GQA_GROUP = N_HEADS // N_KV_HEADS
WINDOW = 128
D_ATTN = N_HEADS * HEAD_DIM
D_KV = N_KV_HEADS * HEAD_DIM
D_LRU = 256
N_LRU_BLOCKS = 4
LRU_BLOCK = D_LRU // N_LRU_BLOCKS
LRU_CONV = 4
LRU_C = 8.0
D_SC = 256
SC_CONV = 3
D_MIX = D_ATTN + D_LRU + D_SC
D_IN = D_ATTN + 2 * D_KV + 2 * D_LRU + 3 * D_SC
D_FF = 2816
FFN_CONV = 3
EPS = 1e-6

_Q0, _K0, _V0 = 0, D_ATTN, D_ATTN + D_KV
_LX0 = D_ATTN + 2 * D_KV
_LG0 = _LX0 + D_LRU
_SB0 = _LG0 + D_LRU
_SC0 = _SB0 + D_SC
_SU0 = _SC0 + D_SC

SUBLANES = 8
LANES = 128
TOKEN_BLOCK = 512
FF_CHUNK = 256
N_FF_CHUNKS = D_FF // FF_CHUNK
SEQ_GROUP = 8
D_QPAD = N_HEADS * D_KV
VMEM_LIMIT_BYTES = 56 * 1024 * 1024

_F32 = jnp.float32
_BF16 = jnp.bfloat16
_NT = (((1,), (1,)), ((), ()))


def _rms(x, g):
    ms = jnp.mean(x * x, axis=-1, keepdims=True)
    return x * lax.rsqrt(ms + EPS) * g


def _softplus(x):
    return jnp.maximum(x, 0.0) + jnp.log1p(jnp.exp(-jnp.abs(x)))


def _lru_coeffs(xc, gates, lam):
    r = jax.nn.sigmoid(gates[:, :D_LRU])
    i = jax.nn.sigmoid(gates[:, D_LRU:])
    log_a = (-LRU_C) * r * _softplus(-lam)
    a = jnp.exp(log_a)
    b = jnp.sqrt(jnp.tanh(-log_a) * (1.0 + a * a)) * (i * xc)
    return a, b


def _scan_rows(a, b):
    n = a.shape[0]
    rows = lax.broadcasted_iota(jnp.int32, a.shape, 0)
    d = 1
    while d < SUBLANES:
        keep = rows >= d
        a_prev = jnp.where(keep, pltpu.roll(a, d, 0), 1.0)
        b_prev = jnp.where(keep, pltpu.roll(b, d, 0), 0.0)
        b = a * b_prev + b
        a = a * a_prev
        d *= 2
    while d < n:
        b = jnp.concatenate([b[:d], a[d:] * b[:-d] + b[d:]], axis=0)
        a = jnp.concatenate([a[:d], a[d:] * a[:-d]], axis=0)
        d *= 2
    return a, b


def _causal_conv(u, w_ref, col0, scr, hist=None):
    n, c = u.shape
    taps = w_ref.shape[0]
    h0 = SUBLANES - (taps - 1)
    outs, tails = [], []
    for j in range(c // LANES):
        wcols = slice(col0 + j * LANES, col0 + (j + 1) * LANES)
        uj = u[:, j * LANES:(j + 1) * LANES]
        scr[j, SUBLANES:SUBLANES + n, :] = uj
        if hist is not None:
            scr[j, h0:SUBLANES, :] = hist[j, h0:SUBLANES, :]
        y = w_ref[taps - 1:taps, wcols] * uj
        for k in range(taps - 1):
            y = y + w_ref[k:k + 1, wcols] * scr[j, h0 + k:h0 + k + n, :]
        tail = scr[j, n + h0:n + SUBLANES, :]
        (scr if hist is None else hist)[j, h0:SUBLANES, :] = tail
        outs.append(y)
        tails.append(tail)
    return jnp.concatenate(outs, axis=-1), jnp.concatenate(tails, axis=-1)


def _prompt_layer_kernel(
        sinks_ref, x_ref, g_mix_ref, w_in_ref, lcw_ref, lcb_ref, w_gates_ref, b_gates_ref,
        lam_ref, scw_ref, g_out_ref, w_out_ref, g_ffn_ref, w_up_ref, fcw_ref, w_down_ref,
        g_final_ref,
        y_ref, kv_out_ref, h_out_ref, lbuf_out_ref, sbuf_out_ref, fbuf_out_ref,
        kv_scr, lc_scr, sc_scr, h_scr, up_scr, fhist_scr, x1_scr, hn_scr, act_scr,
        *, layer, final):
    tb = TOKEN_BLOCK
    t = pl.program_id(1)

    @pl.when(t == 0)
    def _():
        kv_scr[0:WINDOW, :] = jnp.zeros((WINDOW, 2 * D_KV), _BF16)
        lc_scr[:, 0:SUBLANES, :] = jnp.zeros((D_LRU // LANES, SUBLANES, LANES), _F32)
        sc_scr[:, 0:SUBLANES, :] = jnp.zeros((D_SC // LANES, SUBLANES, LANES), _F32)
        h_scr[...] = jnp.zeros_like(h_scr)
        fhist_scr[...] = jnp.zeros_like(fhist_scr)

    @pl.when(t > 0)
    def _():
        kv_scr[0:WINDOW, :] = kv_scr[tb:tb + WINDOW, :]

    x = x_ref[0]
    hb = _rms(x, g_mix_ref[...]).astype(_BF16)
    z = jnp.dot(hb, w_in_ref[...], preferred_element_type=_F32)

    q = (z[:, _Q0:_Q0 + D_ATTN] * (HEAD_DIM ** -0.5)).astype(_BF16)
    kv_scr[WINDOW:WINDOW + tb, :] = z[:, _K0:_K0 + 2 * D_KV].astype(_BF16)
    kv_out_ref[0] = z[tb - WINDOW:tb, _K0:_K0 + 2 * D_KV]

    qi = lax.broadcasted_iota(jnp.int32, (WINDOW, 2 * WINDOW), 0)
    kj = lax.broadcasted_iota(jnp.int32, (WINDOW, 2 * WINDOW), 1)
    band = (kj >= qi) & (kj <= qi + WINDOW)
    band_first = band & (kj >= WINDOW * (1 - jnp.minimum(t, 1)))

    a_blocks = []
    for j in range(tb // WINDOW):
        keys = kv_scr[j * WINDOW:(j + 2) * WINDOW, :]
        qj = q[j * WINDOW:(j + 1) * WINDOW, :]
        mask = band_first if j == 0 else band
        outs = []
        for h in range(N_HEADS):
            g = h // GQA_GROUP
            sink = sinks_ref[layer, h]
            s = lax.dot_general(qj[:, h * HEAD_DIM:(h + 1) * HEAD_DIM],
                                keys[:, g * HEAD_DIM:(g + 1) * HEAD_DIM], _NT,
                                preferred_element_type=_F32)
            s = jnp.where(mask, s, -jnp.inf)
            m = jnp.maximum(jnp.max(s, axis=-1, keepdims=True), sink)
            p = jnp.exp(s - m)
            denom = jnp.sum(p, axis=-1, keepdims=True) + jnp.exp(sink - m)
            o = jnp.dot(p.astype(_BF16),
                        keys[:, D_KV + g * HEAD_DIM:D_KV + (g + 1) * HEAD_DIM],
                        preferred_element_type=_F32)
            outs.append(o / denom)
        a_blocks.append(jnp.concatenate(outs, axis=-1))
    a_out = jnp.concatenate(a_blocks, axis=0)

    xc, tail = _causal_conv(z[:, _LX0:_LX0 + D_LRU], lcw_ref, 0, lc_scr)
    xc = xc + lcb_ref[...]
    lbuf_out_ref[0] = tail

    gates = jnp.dot(xc.astype(_BF16), w_gates_ref[...],
                    preferred_element_type=_F32) + b_gates_ref[...]
    a_dec, b_in = _lru_coeffs(xc, gates, lam_ref[...])
    carry = h_scr[0:1, :]
    h_blocks = []
    for j in range(tb // WINDOW):
        a_cum, b_cum = _scan_rows(a_dec[j * WINDOW:(j + 1) * WINDOW],
                                  b_in[j * WINDOW:(j + 1) * WINDOW])
        hj = a_cum * carry + b_cum
        carry = hj[WINDOW - 1:WINDOW, :]
        h_blocks.append(hj)
    hseq = jnp.concatenate(h_blocks, axis=0)
    h_scr[0:1, :] = carry
    h_out_ref[0] = carry
    b_out = hseq * jax.nn.gelu(z[:, _LG0:_LG0 + D_LRU])

    cu = z[:, _SC0:_SC0 + D_SC] * z[:, _SU0:_SU0 + D_SC]
    cconv, tail = _causal_conv(cu, scw_ref, 0, sc_scr)
    sbuf_out_ref[0] = tail
    c_out = z[:, _SB0:_SB0 + D_SC] * cconv

    g_out = g_out_ref[...]
    merged = jnp.concatenate([
        _rms(a_out, g_out[:, :D_ATTN]),
        _rms(b_out, g_out[:, D_ATTN:D_ATTN + D_LRU]),
        _rms(c_out, g_out[:, D_ATTN + D_LRU:])], axis=-1).astype(_BF16)
    x1 = x + jnp.dot(merged, w_out_ref[...], preferred_element_type=_F32)

    hn_scr[...] = _rms(x1, g_ffn_ref[...]).astype(_BF16)
    x1_scr[...] = x1
    n_t = FF_CHUNK // LANES
    for c in range(N_FF_CHUNKS):
        cu0, cg0 = c * FF_CHUNK, D_FF + c * FF_CHUNK
        slot = c % 2
        up_u = jnp.dot(hn_scr[...], w_up_ref[:, cu0:cu0 + FF_CHUNK], preferred_element_type=_F32)
        up_g = jnp.dot(hn_scr[...], w_up_ref[:, cg0:cg0 + FF_CHUNK], preferred_element_type=_F32)
        uc, tail_u = _causal_conv(up_u, fcw_ref, cu0, up_scr.at[slot, 0:n_t],
                                  hist=fhist_scr.at[c, 0:n_t])
        gc, tail_g = _causal_conv(up_g, fcw_ref, cg0, up_scr.at[slot, n_t:2 * n_t],
                                  hist=fhist_scr.at[c, n_t:2 * n_t])
        fbuf_out_ref[0, :, cu0:cu0 + FF_CHUNK] = tail_u
        fbuf_out_ref[0, :, cg0:cg0 + FF_CHUNK] = tail_g
        act_scr[:, cu0:cu0 + FF_CHUNK] = (jax.nn.silu(gc) * uc).astype(_BF16)
    x2 = x1_scr[...] + jnp.dot(act_scr[...], w_down_ref[...], preferred_element_type=_F32)
    y_ref[0] = _rms(x2, g_final_ref[...]) if final else x2


def _layer_spec(a, layer):
    zeros = (0,) * (a.ndim - 1)
    return pl.BlockSpec((None,) + a.shape[1:], lambda *_: (layer,) + zeros,
                        pipeline_mode=pl.Buffered(1))


def _prompt_layer(layer, x, sinks, g_mix, w_in, lcw, lcb, w_gates, b_gates, lam, scw, g_out, w_out,
                  g_ffn, w_up, fcw, w_down, g_final, *, final):
    batch, seq, _ = x.shape
    tb = TOKEN_BLOCK
    stacked = (g_mix, w_in, lcw, lcb, w_gates, b_gates, lam, scw, g_out, w_out, g_ffn, w_up,
               fcw, w_down)
    in_specs = [pl.BlockSpec(memory_space=pltpu.SMEM),
                pl.BlockSpec((1, tb, D_MODEL), lambda b, t: (b, t, 0))]
    in_specs += [_layer_spec(a, layer) for a in stacked]
    in_specs += [pl.BlockSpec(g_final.shape, lambda b, t: (0, 0), pipeline_mode=pl.Buffered(1))]
    out_shape = (
        jax.ShapeDtypeStruct((batch, seq, D_MODEL), _F32),
        jax.ShapeDtypeStruct((batch, WINDOW, 2 * D_KV), _F32),
        jax.ShapeDtypeStruct((batch, 1, D_LRU), _F32),
        jax.ShapeDtypeStruct((batch, LRU_CONV - 1, D_LRU), _F32),
        jax.ShapeDtypeStruct((batch, SC_CONV - 1, D_SC), _F32),
        jax.ShapeDtypeStruct((batch, FFN_CONV - 1, 2 * D_FF), _F32),
    )
    out_specs = (
        pl.BlockSpec((1, tb, D_MODEL), lambda b, t: (b, t, 0)),
        pl.BlockSpec((1, WINDOW, 2 * D_KV), lambda b, t: (b, 0, 0)),
        pl.BlockSpec((1, 1, D_LRU), lambda b, t: (b, 0, 0)),
        pl.BlockSpec((1, LRU_CONV - 1, D_LRU), lambda b, t: (b, 0, 0)),
        pl.BlockSpec((1, SC_CONV - 1, D_SC), lambda b, t: (b, 0, 0)),
        pl.BlockSpec((1, FFN_CONV - 1, 2 * D_FF), lambda b, t: (b, 0, 0)),
    )
    n_t = 2 * FF_CHUNK // LANES
    scratch = [
        pltpu.VMEM((WINDOW + tb, 2 * D_KV), _BF16),
        pltpu.VMEM((D_LRU // LANES, SUBLANES + tb, LANES), _F32),
        pltpu.VMEM((D_SC // LANES, SUBLANES + tb, LANES), _F32),
        pltpu.VMEM((SUBLANES, D_LRU), _F32),
        pltpu.VMEM((2, n_t, SUBLANES + tb, LANES), _F32),
        pltpu.VMEM((N_FF_CHUNKS, n_t, SUBLANES, LANES), _F32),
        pltpu.VMEM((tb, D_MODEL), _F32),
        pltpu.VMEM((tb, D_MODEL), _BF16),
        pltpu.VMEM((tb, D_FF), _BF16),
    ]
    return pl.pallas_call(
        functools.partial(_prompt_layer_kernel, layer=layer, final=final),
        grid=(batch, seq // tb),
        in_specs=in_specs,
        out_specs=out_specs,
        out_shape=out_shape,
        scratch_shapes=scratch,
        compiler_params=pltpu.CompilerParams(
            dimension_semantics=("arbitrary", "arbitrary"),
            vmem_limit_bytes=VMEM_LIMIT_BYTES),
        name="prompt_layer",
    )(sinks, x, *stacked, g_final)


def _sample_kernel(
        x_ref, ck_ref, cv_ref, h0_ref, lbuf_ref, sbuf_ref, fst_ref, sink_rows_ref,
        g_mix_ref, w_q_ref, w_in_ref, w_kvt_ref, lcw_ref, lcb_ref, w_gates_ref, b_gates_ref,
        lam_ref, scw_ref, g_out_ref, w_out_ref, g_ffn_ref, w_up_ref, fcw_ref, w_down_ref,
        g_final_ref,
        y_ref, ck_out_ref, cv_out_ref, h_out_ref, lbuf_out_ref, sbuf_out_ref, fst_out_ref,
        xs_scr, hb_scr, hbf_scr, q_scr, rest_scr, o_scr, uc_scr, acc_scr,
        *, n_groups):
    l = pl.program_id(0)
    s = pl.program_id(1)
    gs = SEQ_GROUP
    n_blocks = 2 * N_FF_CHUNKS
    last = n_groups + n_blocks - 1

    @pl.when((l == 0) & (s == 0))
    def _():
        xs_scr[...] = x_ref[...]

    @pl.when(s == 0)
    def _():
        hb = _rms(xs_scr[...], g_mix_ref[...]).astype(_BF16)
        hb_scr[...] = hb
        hbf_scr[...] = hb.astype(_F32)
        q_scr[...] = jnp.dot(hb, w_q_ref[...], preferred_element_type=_F32) * (HEAD_DIM ** -0.5)
        rest_scr[...] = jnp.dot(hb, w_in_ref[:, D_ATTN:], preferred_element_type=_F32)

    @pl.when(s < n_groups)
    def _():
        row0 = pl.multiple_of(s * gs, gs)
        q_all = jnp.concatenate(
            [q_scr[pl.ds(row0, gs), h * D_KV:(h + 1) * D_KV] for h in range(N_HEADS)], axis=0)
        k_new = rest_scr[pl.ds(row0, gs), 0:D_KV]
        v_new = rest_scr[pl.ds(row0, gs), D_KV:2 * D_KV]
        k_new_rows = jnp.concatenate([k_new] * N_HEADS, axis=0)
        v_new_rows = jnp.concatenate([v_new] * N_HEADS, axis=0)

        keys_t = jnp.concatenate([ck_ref[b] for b in range(gs)], axis=-1).astype(_BF16)
        s_all = jnp.dot(q_all.astype(_BF16), keys_t, preferred_element_type=_F32)
        seq_of_row = lax.broadcasted_iota(jnp.int32, (N_HEADS * gs, WINDOW), 0) & (gs - 1)
        sc = jnp.zeros((N_HEADS * gs, WINDOW), _F32)
        for j in range(gs):
            sc = sc + jnp.where(seq_of_row == j, s_all[:, j * WINDOW:(j + 1) * WINDOW], 0.0)
        s_new = jnp.sum(q_all * k_new_rows, axis=-1, keepdims=True)
        sink = sink_rows_ref[:, 0:1]
        m = jnp.maximum(jnp.maximum(jnp.max(sc, axis=-1, keepdims=True), s_new), sink)
        p = jnp.exp(sc - m)
        p_new = jnp.exp(s_new - m)
        denom = jnp.sum(p, axis=-1, keepdims=True) + p_new + jnp.exp(sink - m)
        p_blk = jnp.concatenate(
            [jnp.where(seq_of_row == j, p, 0.0) for j in range(gs)], axis=-1).astype(_BF16)
        vals_t = jnp.concatenate([cv_ref[b] for b in range(gs)], axis=-1).astype(_BF16)
        o = lax.dot_general(p_blk, vals_t, _NT, preferred_element_type=_F32) + p_new * v_new_rows
        o = o / denom
        row = lax.broadcasted_iota(jnp.int32, (N_HEADS * gs, D_KV), 0)
        lane = lax.broadcasted_iota(jnp.int32, (N_HEADS * gs, D_KV), 1)
        own_kv = (lax.shift_right_logical(row, (GQA_GROUP * gs).bit_length() - 1)
                  == lax.shift_right_logical(lane, HEAD_DIM.bit_length() - 1))
        o = jnp.where(own_kv, o, 0.0)
        for h in range(N_HEADS):
            o_scr[pl.ds(row0, gs), h * D_KV:(h + 1) * D_KV] = o[h * gs:(h + 1) * gs, :]

        hb_g = hbf_scr[pl.ds(row0, gs), :].astype(_BF16)
        kv_new_t = lax.dot_general(w_kvt_ref[...], hb_g, _NT, preferred_element_type=_F32)
        pos = lax.broadcasted_iota(jnp.int32, (D_KV, WINDOW), 1)
        for b in range(gs):
            ck_out_ref[b] = jnp.where(pos == WINDOW - 1, kv_new_t[0:D_KV, b:b + 1],
                                      pltpu.roll(ck_ref[b], WINDOW - 1, 1))
            cv_out_ref[b] = jnp.where(pos == WINDOW - 1, kv_new_t[D_KV:2 * D_KV, b:b + 1],
                                      pltpu.roll(cv_ref[b], WINDOW - 1, 1))

    @pl.when(s == n_groups - 1)
    def _():
        rest = rest_scr[...]
        base = 2 * D_KV
        lx = rest[:, base:base + D_LRU]
        lg = rest[:, base + D_LRU:base + 2 * D_LRU]
        sb = rest[:, base + 2 * D_LRU:base + 2 * D_LRU + D_SC]
        scc = rest[:, base + 2 * D_LRU + D_SC:base + 2 * D_LRU + 2 * D_SC]
        su = rest[:, base + 2 * D_LRU + 2 * D_SC:]
        xc = lcb_ref[...] + lcw_ref[LRU_CONV - 1:LRU_CONV, :] * lx
        for k in range(LRU_CONV - 1):
            xc = xc + lcw_ref[k:k + 1, :] * lbuf_ref[k]
        for k in range(LRU_CONV - 2):
            lbuf_out_ref[k] = lbuf_ref[k + 1]
        lbuf_out_ref[LRU_CONV - 2] = lx
        gates = jnp.dot(xc.astype(_BF16), w_gates_ref[...],
                        preferred_element_type=_F32) + b_gates_ref[...]
        a_dec, b_in = _lru_coeffs(xc, gates, lam_ref[...])
        h_new = a_dec * h0_ref[...] + b_in
        h_out_ref[...] = h_new
        b_out = h_new * jax.nn.gelu(lg)
        cu = scc * su
        cconv = scw_ref[SC_CONV - 1:SC_CONV, :] * cu
        for k in range(SC_CONV - 1):
            cconv = cconv + scw_ref[k:k + 1, :] * sbuf_ref[:, k, :]
        for k in range(SC_CONV - 2):
            sbuf_out_ref[:, k, :] = sbuf_ref[:, k + 1, :]
        sbuf_out_ref[:, SC_CONV - 2, :] = cu
        c_out = sb * cconv
        g_out = g_out_ref[...]
        o_pad = o_scr[...]
        a_scale = lax.rsqrt(jnp.sum(o_pad * o_pad, axis=-1, keepdims=True) / D_ATTN + EPS)
        merged = jnp.concatenate([
            o_pad * a_scale * g_out[:, :D_QPAD],
            _rms(b_out, g_out[:, D_QPAD:D_QPAD + D_LRU]),
            _rms(c_out, g_out[:, D_QPAD + D_LRU:])], axis=-1).astype(_BF16)
        x1 = xs_scr[...] + jnp.dot(merged, w_out_ref[...], preferred_element_type=_F32)
        hb_scr[...] = _rms(x1, g_ffn_ref[...]).astype(_BF16)
        acc_scr[...] = x1

    @pl.when(s >= n_groups)
    def _():
        j = s - n_groups
        up = jnp.dot(hb_scr[...], w_up_ref[...], preferred_element_type=_F32)
        upc = fcw_ref[FFN_CONV - 1:FFN_CONV, :] * up
        for k in range(FFN_CONV - 1):
            upc = upc + fcw_ref[k:k + 1, :] * fst_ref[:, k, :]
        for k in range(FFN_CONV - 2):
            fst_out_ref[:, k, :] = fst_ref[:, k + 1, :]
        fst_out_ref[:, FFN_CONV - 2, :] = up

        @pl.when(j < N_FF_CHUNKS)
        def _():
            uc_scr[j] = upc

        @pl.when(j >= N_FF_CHUNKS)
        def _():
            act = (jax.nn.silu(upc) * uc_scr[j - N_FF_CHUNKS]).astype(_BF16)
            acc_scr[...] += jnp.dot(act, w_down_ref[...], preferred_element_type=_F32)

    @pl.when(s == last)
    def _():
        xs_scr[...] = acc_scr[...]

    @pl.when((s == last) & (l == DEPTH - 1))
    def _():
        y_ref[...] = _rms(acc_scr[...], g_final_ref[...])


def _sample_step(x, ck_t, cv_t, h0, lbuf, sbuf, fstate, sink_rows, g_mix, w_q, w_in, w_kvt, lcw,
                 lcb, w_gates, b_gates, lam, scw, g_out_pad, w_out_pad, g_ffn, w_up, fcw, w_down,
                 g_final):
    n_seq = x.shape[0]
    gs = SEQ_GROUP
    n_groups = n_seq // gs
    n_blocks = 2 * N_FF_CHUNKS
    steps = n_groups + n_blocks

    def per_layer(a):
        zeros = (0,) * (a.ndim - 1)
        return pl.BlockSpec((None,) + a.shape[1:], lambda l, s: (l,) + zeros)

    def group(l, s):
        return (l, jnp.minimum(s, n_groups - 1), 0, 0)

    def ff_block(l, s):
        return jnp.clip(s - n_groups, 0, n_blocks - 1)

    cache_spec = pl.BlockSpec((None, gs, D_KV, WINDOW), group)
    fst_spec = pl.BlockSpec((None, n_seq, FFN_CONV - 1, FF_CHUNK), lambda l, s: (l, 0, 0, ff_block(l, s)))
    in_specs = [
        pl.BlockSpec(x.shape, lambda l, s: (0, 0)),
        cache_spec, cache_spec, per_layer(h0), per_layer(lbuf), per_layer(sbuf), fst_spec,
        per_layer(sink_rows), per_layer(g_mix), per_layer(w_q), per_layer(w_in), per_layer(w_kvt),
        per_layer(lcw), per_layer(lcb), per_layer(w_gates), per_layer(b_gates), per_layer(lam),
        per_layer(scw), per_layer(g_out_pad), per_layer(w_out_pad), per_layer(g_ffn),
        pl.BlockSpec((None, D_MODEL, FF_CHUNK), lambda l, s: (l, 0, ff_block(l, s))),
        pl.BlockSpec((None, FFN_CONV, FF_CHUNK), lambda l, s: (l, 0, ff_block(l, s))),
        pl.BlockSpec((None, FF_CHUNK, D_MODEL),
                     lambda l, s: (l, jnp.clip(s - n_groups - N_FF_CHUNKS, 0, N_FF_CHUNKS - 1), 0)),
        pl.BlockSpec(g_final.shape, lambda l, s: (0, 0)),
    ]
    out_shape = (
        jax.ShapeDtypeStruct((n_seq, D_MODEL), _F32),
        jax.ShapeDtypeStruct(ck_t.shape, _F32),
        jax.ShapeDtypeStruct(cv_t.shape, _F32),
        jax.ShapeDtypeStruct(h0.shape, _F32),
        jax.ShapeDtypeStruct(lbuf.shape, _F32),
        jax.ShapeDtypeStruct(sbuf.shape, _F32),
        jax.ShapeDtypeStruct(fstate.shape, _F32),
    )
    out_specs = (
        pl.BlockSpec((n_seq, D_MODEL), lambda l, s: (0, 0)),
        cache_spec, cache_spec, per_layer(h0), per_layer(lbuf), per_layer(sbuf), fst_spec,
    )
    scratch = [
        pltpu.VMEM((n_seq, D_MODEL), _F32),
        pltpu.VMEM((n_seq, D_MODEL), _BF16),
        pltpu.VMEM((n_seq, D_MODEL), _F32),
        pltpu.VMEM((n_seq, D_QPAD), _F32),
        pltpu.VMEM((n_seq, D_IN - D_ATTN), _F32),
        pltpu.VMEM((n_seq, D_QPAD), _F32),
        pltpu.VMEM((N_FF_CHUNKS, n_seq, FF_CHUNK), _F32),
        pltpu.VMEM((n_seq, D_MODEL), _F32),
    ]
    return pl.pallas_call(
        functools.partial(_sample_kernel, n_groups=n_groups),
        grid=(DEPTH, steps),
        in_specs=in_specs,
        out_specs=out_specs,
        out_shape=out_shape,
        scratch_shapes=scratch,
        compiler_params=pltpu.CompilerParams(
            dimension_semantics=("arbitrary", "arbitrary"),
            vmem_limit_bytes=VMEM_LIMIT_BYTES),
        name="sample_step",
    )(x, ck_t, cv_t, h0, lbuf, sbuf, fstate, sink_rows, g_mix, w_q, w_in, w_kvt, lcw, lcb, w_gates,
      b_gates, lam, scw, g_out_pad, w_out_pad, g_ffn, w_up, fcw, w_down, g_final)


def _block_diag(w):
    eye = jnp.eye(N_LRU_BLOCKS, dtype=w.dtype).reshape(1, N_LRU_BLOCKS, 1, N_LRU_BLOCKS, 1)
    return (w[:, :, :, None, :] * eye).reshape(DEPTH, D_LRU, D_LRU)


def _pad_heads(w, axis):
    w = jnp.moveaxis(w, axis, -1)
    lead = w.shape[:-1]
    w = w.reshape(lead + (N_KV_HEADS, GQA_GROUP, 1, HEAD_DIM))
    eye = jnp.eye(N_KV_HEADS, dtype=w.dtype).reshape(N_KV_HEADS, 1, N_KV_HEADS, 1)
    w = (w * eye).reshape(lead + (D_QPAD,))
    return jnp.moveaxis(w, -1, axis)


def kernel(x_prompt, x_sample, cache_k, cache_v, state_rglru, state_lru_conv, state_sconv, state_ffn_conv, g_mix, w_in, sinks, lru_conv_w, lru_conv_b, lru_wa, lru_ba, lru_wi, lru_bi, lru_lambda, sc_conv_w, g_out, w_out, g_ffn, w_up, ffn_conv_w, w_down, g_final):
    batch = x_prompt.shape[0]
    n_seq = x_sample.shape[0]

    w_in_b = w_in.astype(_BF16)
    w_out_b = w_out.astype(_BF16)
    w_up_b = w_up.astype(_BF16)
    w_down_b = w_down.astype(_BF16)
    w_gates = jnp.concatenate([_block_diag(lru_wa), _block_diag(lru_wi)], axis=-1).astype(_BF16)
    b_gates = jnp.concatenate([lru_ba, lru_bi], axis=-1)[:, None, :]
    w_q_pad = _pad_heads(w_in[:, :, :D_ATTN], 2).astype(_BF16)
    w_kv = lax.optimization_barrier(w_in[:, :, _K0:_K0 + 2 * D_KV].astype(_BF16))
    w_kvt = jnp.swapaxes(w_kv, 1, 2)
    w_out_pad = jnp.concatenate([_pad_heads(w_out[:, :D_ATTN], 1), w_out[:, D_ATTN:]],
                                axis=1).astype(_BF16)
    g_out_pad = jnp.concatenate([_pad_heads(g_out[:, :D_ATTN], 1), g_out[:, D_ATTN:]], axis=1)
    sink_rows = jnp.broadcast_to(
        jnp.repeat(sinks, SEQ_GROUP, axis=1)[:, :, None], (DEPTH, N_HEADS * SEQ_GROUP, D_KV))

    row = lambda a: a[:, None, :]
    g_mix_r, g_out_r, g_ffn_r = row(g_mix), row(g_out), row(g_ffn)
    lcb_r, lam_r, g_out_pad_r = row(lru_conv_b), row(lru_lambda), row(g_out_pad)
    g_final_r = g_final[None, :]

    ck_t = jnp.transpose(cache_k, (0, 1, 3, 4, 2)).reshape(DEPTH, n_seq, D_KV, WINDOW)
    cv_t = jnp.transpose(cache_v, (0, 1, 3, 4, 2)).reshape(DEPTH, n_seq, D_KV, WINDOW)
    lbuf_t = jnp.transpose(state_lru_conv, (0, 2, 1, 3))
    ys, ck_new, cv_new, h_new, lbuf_new, sbuf_new, fst_new = _sample_step(
        x_sample.reshape(n_seq, D_MODEL), ck_t, cv_t, state_rglru, lbuf_t, state_sconv,
        state_ffn_conv, sink_rows, g_mix_r, w_q_pad, w_in_b, w_kvt, lru_conv_w, lcb_r, w_gates,
        b_gates, lam_r, sc_conv_w, g_out_pad_r, w_out_pad, g_ffn_r, w_up_b, ffn_conv_w, w_down_b,
        g_final_r)
    unview = lambda c: jnp.transpose(
        c.reshape(DEPTH, n_seq, N_KV_HEADS, HEAD_DIM, WINDOW), (0, 1, 4, 2, 3))
    s_out = (unview(ck_new), unview(cv_new), h_new, jnp.transpose(lbuf_new, (0, 2, 1, 3)),
             sbuf_new, fst_new)

    xp = x_prompt
    p_states = []
    for l in range(DEPTH):
        xp, kv_new, h_last, lbuf_p, sbuf_p, fbuf_p = _prompt_layer(
            l, xp, sinks, g_mix_r, w_in_b, lru_conv_w, lcb_r, w_gates, b_gates, lam_r, sc_conv_w,
            g_out_r, w_out_b, g_ffn_r, w_up_b, ffn_conv_w, w_down_b, g_final_r,
            final=(l == DEPTH - 1))
        p_states.append((
            kv_new[:, :, :D_KV].reshape(batch, WINDOW, N_KV_HEADS, HEAD_DIM),
            kv_new[:, :, D_KV:].reshape(batch, WINDOW, N_KV_HEADS, HEAD_DIM),
            h_last.reshape(batch, D_LRU), lbuf_p, sbuf_p, fbuf_p))

    stack = lambda i: jnp.stack([st[i] for st in p_states])
    return ((xp, ys.reshape(n_seq, 1, D_MODEL)) + tuple(stack(i) for i in range(6)) + s_out)
```

```python
import functools

import jax
import jax.numpy as jnp
from jax import lax
from jax.experimental import pallas as pl
from jax.experimental.pallas import tpu as pltpu

D_MODEL = 1024
DEPTH = 4
HEAD_DIM = 64
N_HEADS = 8
N_KV_HEADS = 2
GQA_GROUP = N_HEADS // N_KV_HEADS
WINDOW = 128
D_ATTN = N_HEADS * HEAD_DIM
D_KV = N_KV_HEADS * HEAD_DIM
D_LRU = 256
N_LRU_BLOCKS = 4
LRU_BLOCK = D_LRU // N_LRU_BLOCKS
LRU_CONV = 4
LRU_C = 8.0
D_SC = 256
SC_CONV = 3
D_MIX = D_ATTN + D_LRU + D_SC
D_IN = D_ATTN + 2 * D_KV + 2 * D_LRU + 3 * D_SC
D_FF = 2816
FFN_CONV = 3
EPS = 1e-6

_Q0, _K0, _V0 = 0, D_ATTN, D_ATTN + D_KV
_LX0 = D_ATTN + 2 * D_KV
_LG0 = _LX0 + D_LRU
_SB0 = _LG0 + D_LRU
_SC0 = _SB0 + D_SC
_SU0 = _SC0 + D_SC

SUBLANES = 8
LANES = 128
TOKEN_BLOCK = 512
FF_CHUNK = 256
N_FF_CHUNKS = D_FF // FF_CHUNK
SEQ_GROUP = 8
D_QPAD = N_HEADS * D_KV
VMEM_LIMIT_BYTES = 56 * 1024 * 1024

_F32 = jnp.float32
_BF16 = jnp.bfloat16
_NT = (((1,), (1,)), ((), ()))


def _rms(x, g):
    ms = jnp.mean(x * x, axis=-1, keepdims=True)
    return x * lax.rsqrt(ms + EPS) * g


def _softplus(x):
    return jnp.maximum(x, 0.0) + jnp.log1p(jnp.exp(-jnp.abs(x)))


def _lru_coeffs(xc, gates, lam):
    r = jax.nn.sigmoid(gates[:, :D_LRU])
    i = jax.nn.sigmoid(gates[:, D_LRU:])
    log_a = (-LRU_C) * r * _softplus(-lam)
    a = jnp.exp(log_a)
    b = jnp.sqrt(jnp.tanh(-log_a) * (1.0 + a * a)) * (i * xc)
    return a, b


def _scan_rows(a, b):
    n = a.shape[0]
    rows = lax.broadcasted_iota(jnp.int32, a.shape, 0)
    d = 1
    while d < SUBLANES:
        keep = rows >= d
        a_prev = jnp.where(keep, pltpu.roll(a, d, 0), 1.0)
        b_prev = jnp.where(keep, pltpu.roll(b, d, 0), 0.0)
        b = a * b_prev + b
        a = a * a_prev
        d *= 2
    while d < n:
        b = jnp.concatenate([b[:d], a[d:] * b[:-d] + b[d:]], axis=0)
        a = jnp.concatenate([a[:d], a[d:] * a[:-d]], axis=0)
        d *= 2
    return a, b


def _causal_conv(u, w_ref, col0, scr, hist=None):
    n, c = u.shape
    taps = w_ref.shape[0]
    h0 = SUBLANES - (taps - 1)
    outs, tails = [], []
    for j in range(c // LANES):
        wcols = slice(col0 + j * LANES, col0 + (j + 1) * LANES)
        uj = u[:, j * LANES:(j + 1) * LANES]
        scr[j, SUBLANES:SUBLANES + n, :] = uj
        if hist is not None:
            scr[j, h0:SUBLANES, :] = hist[j, h0:SUBLANES, :]
        y = w_ref[taps - 1:taps, wcols] * uj
        for k in range(taps - 1):
            y = y + w_ref[k:k + 1, wcols] * scr[j, h0 + k:h0 + k + n, :]
        tail = scr[j, n + h0:n + SUBLANES, :]
        (scr if hist is None else hist)[j, h0:SUBLANES, :] = tail
        outs.append(y)
        tails.append(tail)
    return jnp.concatenate(outs, axis=-1), jnp.concatenate(tails, axis=-1)


def _prompt_layer_kernel(
        sinks_ref, x_ref, g_mix_ref, w_in_ref, lcw_ref, lcb_ref, w_gates_ref, b_gates_ref,
        lam_ref, scw_ref, g_out_ref, w_out_ref, g_ffn_ref, w_up_ref, fcw_ref, w_down_ref,
        g_final_ref,
        y_ref, y_last_ref, kv_out_ref, h_out_ref, lbuf_out_ref, sbuf_out_ref, fbuf_out_ref,
        kv_scr, lc_scr, sc_scr, h_scr, up_scr, fhist_scr, x1_scr, hn_scr, act_scr,
        *, layer, final, blocks_per_seq, n_blocks):
    tb = TOKEN_BLOCK
    i = pl.program_id(0)
    t = lax.rem(i, blocks_per_seq)

    def finish_rows(rows):
        x2 = x1_scr[rows, :] + jnp.dot(act_scr[rows, :], w_down_ref[...],
                                       preferred_element_type=_F32)
        return _rms(x2, g_final_ref[...]) if final else x2

    @pl.when(i == 0)
    def _():
        x1_scr[...] = jnp.zeros_like(x1_scr)
        act_scr[...] = jnp.zeros_like(act_scr)

    @pl.when(t == 0)
    def _():
        kv_scr[0:WINDOW, :] = jnp.zeros((WINDOW, 2 * D_KV), _BF16)
        lc_scr[:, 0:SUBLANES, :] = jnp.zeros((D_LRU // LANES, SUBLANES, LANES), _F32)
        sc_scr[:, 0:SUBLANES, :] = jnp.zeros((D_SC // LANES, SUBLANES, LANES), _F32)
        h_scr[...] = jnp.zeros_like(h_scr)
        fhist_scr[...] = jnp.zeros_like(fhist_scr)

    @pl.when(t > 0)
    def _():
        kv_scr[0:WINDOW, :] = kv_scr[tb:tb + WINDOW, :]

    x = x_ref[0]
    hb = _rms(x, g_mix_ref[...]).astype(_BF16)

    zf = jnp.dot(hb, w_in_ref[...], preferred_element_type=_F32)
    z = {c0: zf[:, c0:c0 + D_LRU] for c0 in (_LX0, _LG0, _SB0, _SC0, _SU0)}

    q = (zf[:, _Q0:_Q0 + D_ATTN] * (HEAD_DIM ** -0.5)).astype(_BF16)
    kv_scr[WINDOW:WINDOW + tb, :] = zf[:, _K0:_K0 + 2 * D_KV].astype(_BF16)
    kv_out_ref[0] = zf[tb - WINDOW:tb, _K0:_K0 + 2 * D_KV]

    qi = lax.broadcasted_iota(jnp.int32, (WINDOW, 2 * WINDOW), 0)
    kj = lax.broadcasted_iota(jnp.int32, (WINDOW, 2 * WINDOW), 1)
    band = (kj >= qi) & (kj <= qi + WINDOW)
    band_first = band & (kj >= WINDOW * (1 - jnp.minimum(t, 1)))

    a_blocks = []
    n_sub = tb // WINDOW
    for j in range(n_sub):
        keys = kv_scr[j * WINDOW:(j + 2) * WINDOW, :]
        qj = q[j * WINDOW:(j + 1) * WINDOW, :]
        mask = band_first if j == 0 else band
        outs = []
        for h in range(N_HEADS):
            g = h // GQA_GROUP
            sink = sinks_ref[layer, h]
            s = lax.dot_general(qj[:, h * HEAD_DIM:(h + 1) * HEAD_DIM],
                                keys[:, g * HEAD_DIM:(g + 1) * HEAD_DIM], _NT,
                                preferred_element_type=_F32)
            s = jnp.where(mask, s, -jnp.inf)
            m = jnp.maximum(jnp.max(s, axis=-1, keepdims=True), sink)
            p = jnp.exp(s - m)
            denom = jnp.sum(p, axis=-1, keepdims=True) + jnp.exp(sink - m)
            o = jnp.dot(p.astype(_BF16),
                        keys[:, D_KV + g * HEAD_DIM:D_KV + (g + 1) * HEAD_DIM],
                        preferred_element_type=_F32)
            outs.append(o / denom)
        a_blocks.append(jnp.concatenate(outs, axis=-1))
    a_out = jnp.concatenate(a_blocks, axis=0)

    xc, tail = _causal_conv(z[_LX0], lcw_ref, 0, lc_scr)
    xc = xc + lcb_ref[...]
    lbuf_out_ref[0] = tail

    gates = jnp.dot(xc.astype(_BF16), w_gates_ref[...],
                    preferred_element_type=_F32) + b_gates_ref[...]
    a_dec, b_in = _lru_coeffs(xc, gates, lam_ref[...])
    carry = h_scr[0:1, :]
    h_blocks = []
    for j in range(n_sub):
        rows = slice(j * WINDOW, (j + 1) * WINDOW)
        a_cum, b_cum = _scan_rows(a_dec[rows], b_in[rows])
        hj = a_cum * carry + b_cum
        carry = hj[WINDOW - 1:WINDOW, :]
        h_blocks.append(hj)
        y_ref[0, rows, :] = finish_rows(rows)
    hseq = jnp.concatenate(h_blocks, axis=0)
    h_scr[0:1, :] = carry
    h_out_ref[0] = carry
    b_out = hseq * jax.nn.gelu(z[_LG0])

    cu = z[_SC0] * z[_SU0]
    cconv, tail = _causal_conv(cu, scw_ref, 0, sc_scr)
    sbuf_out_ref[0] = tail
    c_out = z[_SB0] * cconv

    g_out = g_out_ref[...]
    merged = jnp.concatenate([
        _rms(a_out, g_out[:, :D_ATTN]),
        _rms(b_out, g_out[:, D_ATTN:D_ATTN + D_LRU]),
        _rms(c_out, g_out[:, D_ATTN + D_LRU:])], axis=-1).astype(_BF16)
    x1 = x + jnp.dot(merged, w_out_ref[...], preferred_element_type=_F32)

    hn_scr[...] = _rms(x1, g_ffn_ref[...]).astype(_BF16)
    x1_scr[...] = x1
    n_t = FF_CHUNK // LANES
    for c in range(N_FF_CHUNKS):
        cu0, cg0 = c * FF_CHUNK, D_FF + c * FF_CHUNK
        slot = c % 2
        up_u = jnp.dot(hn_scr[...], w_up_ref[:, cu0:cu0 + FF_CHUNK], preferred_element_type=_F32)
        up_g = jnp.dot(hn_scr[...], w_up_ref[:, cg0:cg0 + FF_CHUNK], preferred_element_type=_F32)
        uc, tail_u = _causal_conv(up_u, fcw_ref, cu0, up_scr.at[slot, 0:n_t],
                                  hist=fhist_scr.at[c, 0:n_t])
        gc, tail_g = _causal_conv(up_g, fcw_ref, cg0, up_scr.at[slot, n_t:2 * n_t],
                                  hist=fhist_scr.at[c, n_t:2 * n_t])
        fbuf_out_ref[0, :, cu0:cu0 + FF_CHUNK] = tail_u
        fbuf_out_ref[0, :, cg0:cg0 + FF_CHUNK] = tail_g
        act_scr[:, cu0:cu0 + FF_CHUNK] = (jax.nn.silu(gc) * uc).astype(_BF16)

    @pl.when(i == n_blocks - 1)
    def _():
        y_last_ref[...] = finish_rows(slice(0, tb))


def _layer_spec(a, layer):
    zeros = (0,) * (a.ndim - 1)
    return pl.BlockSpec((None,) + a.shape[1:], lambda *_: (layer,) + zeros,
                        pipeline_mode=pl.Buffered(1))


def _prompt_layer(layer, x, sinks, g_mix, w_in, lcw, lcb, w_gates, b_gates, lam, scw, g_out, w_out,
                  g_ffn, w_up, fcw, w_down, g_final, *, final):
    batch, seq, _ = x.shape
    tb = TOKEN_BLOCK
    stacked = (g_mix, w_in, lcw, lcb, w_gates, b_gates, lam, scw, g_out, w_out, g_ffn, w_up,
               fcw, w_down)
    bps = seq // tb
    n_blocks = batch * bps

    def cur(i):
        return (i // bps, lax.rem(i, bps), 0)

    def prev(i):
        return cur(jnp.maximum(i - 1, 0))

    def seq_of(i):
        return (i // bps, 0, 0)

    in_specs = [pl.BlockSpec(memory_space=pltpu.SMEM),
                pl.BlockSpec((1, tb, D_MODEL), cur)]
    in_specs += [_layer_spec(a, layer) for a in stacked]
    in_specs += [pl.BlockSpec(g_final.shape, lambda i: (0, 0), pipeline_mode=pl.Buffered(1))]
    out_shape = (
        jax.ShapeDtypeStruct((batch, seq, D_MODEL), _F32),
        jax.ShapeDtypeStruct((tb, D_MODEL), _F32),
        jax.ShapeDtypeStruct((batch, WINDOW, 2 * D_KV), _F32),
        jax.ShapeDtypeStruct((batch, 1, D_LRU), _F32),
        jax.ShapeDtypeStruct((batch, LRU_CONV - 1, D_LRU), _F32),
        jax.ShapeDtypeStruct((batch, SC_CONV - 1, D_SC), _F32),
        jax.ShapeDtypeStruct((batch, FFN_CONV - 1, 2 * D_FF), _F32),
    )
    out_specs = (
        pl.BlockSpec((1, tb, D_MODEL), prev),
        pl.BlockSpec((tb, D_MODEL), lambda i: (0, 0)),
        pl.BlockSpec((1, WINDOW, 2 * D_KV), seq_of),
        pl.BlockSpec((1, 1, D_LRU), seq_of),
        pl.BlockSpec((1, LRU_CONV - 1, D_LRU), seq_of),
        pl.BlockSpec((1, SC_CONV - 1, D_SC), seq_of),
        pl.BlockSpec((1, FFN_CONV - 1, 2 * D_FF), seq_of),
    )
    n_t = 2 * FF_CHUNK // LANES
    scratch = [
        pltpu.VMEM((WINDOW + tb, 2 * D_KV), _BF16),
        pltpu.VMEM((D_LRU // LANES, SUBLANES + tb, LANES), _F32),
        pltpu.VMEM((D_SC // LANES, SUBLANES + tb, LANES), _F32),
        pltpu.VMEM((SUBLANES, D_LRU), _F32),
        pltpu.VMEM((2, n_t, SUBLANES + tb, LANES), _F32),
        pltpu.VMEM((N_FF_CHUNKS, n_t, SUBLANES, LANES), _F32),
        pltpu.VMEM((tb, D_MODEL), _F32),
        pltpu.VMEM((tb, D_MODEL), _BF16),
        pltpu.VMEM((tb, D_FF), _BF16),
    ]
    y, y_last, *states = pl.pallas_call(
        functools.partial(_prompt_layer_kernel, layer=layer, final=final, blocks_per_seq=bps,
                          n_blocks=n_blocks),
        grid=(n_blocks,),
        in_specs=in_specs,
        out_specs=out_specs,
        out_shape=out_shape,
        scratch_shapes=scratch,
        compiler_params=pltpu.CompilerParams(
            dimension_semantics=("arbitrary",),
            vmem_limit_bytes=VMEM_LIMIT_BYTES),
        name="prompt_layer",
    )(sinks, x, *stacked, g_final)
    y = lax.dynamic_update_slice(y, y_last[None], (batch - 1, seq - tb, 0))
    return (y, *states)


def _sample_kernel(
        x_ref, ck_ref, cv_ref, h0_ref, lbuf_ref, sbuf_ref, fst_ref, sink_rows_ref,
        g_mix_ref, w_q_ref, w_in_ref, w_kvt_ref, lcw_ref, lcb_ref, w_gates_ref, b_gates_ref,
        lam_ref, scw_ref, g_out_ref, w_out_ref, g_ffn_ref, w_up_ref, fcw_ref, w_down_ref,
        g_final_ref,
        y_ref, ck_out_ref, cv_out_ref, h_out_ref, lbuf_out_ref, sbuf_out_ref, fst_out_ref,
        xs_scr, hb_scr, hbf_scr, q_scr, rest_scr, o_scr, uc_scr, acc_scr,
        *, n_groups):
    l = pl.program_id(0)
    s = pl.program_id(1)
    gs = SEQ_GROUP
    n_blocks = 2 * N_FF_CHUNKS
    last = n_groups + n_blocks - 1

    @pl.when((l == 0) & (s == 0))
    def _():
        xs_scr[...] = x_ref[...]

    @pl.when(s == 0)
    def _():
        hb = _rms(xs_scr[...], g_mix_ref[...]).astype(_BF16)
        hb_scr[...] = hb
        hbf_scr[...] = hb.astype(_F32)
        q_scr[...] = jnp.dot(hb, w_q_ref[...], preferred_element_type=_F32) * (HEAD_DIM ** -0.5)
        rest_scr[...] = jnp.dot(hb, w_in_ref[:, D_ATTN:], preferred_element_type=_F32)

    @pl.when(s < n_groups)
    def _():
        row0 = pl.multiple_of(s * gs, gs)
        q_all = jnp.concatenate(
            [q_scr[pl.ds(row0, gs), h * D_KV:(h + 1) * D_KV] for h in range(N_HEADS)], axis=0)
        k_new = rest_scr[pl.ds(row0, gs), 0:D_KV]
        v_new = rest_scr[pl.ds(row0, gs), D_KV:2 * D_KV]
        k_new_rows = jnp.concatenate([k_new] * N_HEADS, axis=0)
        v_new_rows = jnp.concatenate([v_new] * N_HEADS, axis=0)

        keys_t = jnp.concatenate([ck_ref[b] for b in range(gs)], axis=-1).astype(_BF16)
        s_all = jnp.dot(q_all.astype(_BF16), keys_t, preferred_element_type=_F32)
        seq_of_row = lax.broadcasted_iota(jnp.int32, (N_HEADS * gs, WINDOW), 0) & (gs - 1)
        sc = jnp.zeros((N_HEADS * gs, WINDOW), _F32)
        for j in range(gs):
            sc = sc + jnp.where(seq_of_row == j, s_all[:, j * WINDOW:(j + 1) * WINDOW], 0.0)
        s_new = jnp.sum(q_all * k_new_rows, axis=-1, keepdims=True)
        sink = sink_rows_ref[:, 0:1]
        m = jnp.maximum(jnp.maximum(jnp.max(sc, axis=-1, keepdims=True), s_new), sink)
        p = jnp.exp(sc - m)
        p_new = jnp.exp(s_new - m)
        denom = jnp.sum(p, axis=-1, keepdims=True) + p_new + jnp.exp(sink - m)
        p_blk = jnp.concatenate(
            [jnp.where(seq_of_row == j, p, 0.0) for j in range(gs)], axis=-1).astype(_BF16)
        vals_t = jnp.concatenate([cv_ref[b] for b in range(gs)], axis=-1).astype(_BF16)
        o = lax.dot_general(p_blk, vals_t, _NT, preferred_element_type=_F32) + p_new * v_new_rows
        o = o / denom
        row = lax.broadcasted_iota(jnp.int32, (N_HEADS * gs, D_KV), 0)
        lane = lax.broadcasted_iota(jnp.int32, (N_HEADS * gs, D_KV), 1)
        own_kv = (lax.shift_right_logical(row, (GQA_GROUP * gs).bit_length() - 1)
                  == lax.shift_right_logical(lane, HEAD_DIM.bit_length() - 1))
        o = jnp.where(own_kv, o, 0.0)
        for h in range(N_HEADS):
            o_scr[pl.ds(row0, gs), h * D_KV:(h + 1) * D_KV] = o[h * gs:(h + 1) * gs, :]

        hb_g = hbf_scr[pl.ds(row0, gs), :].astype(_BF16)
        kv_new_t = lax.dot_general(w_kvt_ref[...], hb_g, _NT, preferred_element_type=_F32)
        pos = lax.broadcasted_iota(jnp.int32, (D_KV, WINDOW), 1)
        for b in range(gs):
            ck_out_ref[b] = jnp.where(pos == WINDOW - 1, kv_new_t[0:D_KV, b:b + 1],
                                      pltpu.roll(ck_ref[b], WINDOW - 1, 1))
            cv_out_ref[b] = jnp.where(pos == WINDOW - 1, kv_new_t[D_KV:2 * D_KV, b:b + 1],
                                      pltpu.roll(cv_ref[b], WINDOW - 1, 1))

    @pl.when(s == n_groups - 1)
    def _():
        rest = rest_scr[...]
        base = 2 * D_KV
        lx = rest[:, base:base + D_LRU]
        lg = rest[:, base + D_LRU:base + 2 * D_LRU]
        sb = rest[:, base + 2 * D_LRU:base + 2 * D_LRU + D_SC]
        scc = rest[:, base + 2 * D_LRU + D_SC:base + 2 * D_LRU + 2 * D_SC]
        su = rest[:, base + 2 * D_LRU + 2 * D_SC:]
        xc = lcb_ref[...] + lcw_ref[LRU_CONV - 1:LRU_CONV, :] * lx
        for k in range(LRU_CONV - 1):
            xc = xc + lcw_ref[k:k + 1, :] * lbuf_ref[k]
        for k in range(LRU_CONV - 2):
            lbuf_out_ref[k] = lbuf_ref[k + 1]
        lbuf_out_ref[LRU_CONV - 2] = lx
        gates = jnp.dot(xc.astype(_BF16), w_gates_ref[...],
                        preferred_element_type=_F32) + b_gates_ref[...]
        a_dec, b_in = _lru_coeffs(xc, gates, lam_ref[...])
        h_new = a_dec * h0_ref[...] + b_in
        h_out_ref[...] = h_new
        b_out = h_new * jax.nn.gelu(lg)
        cu = scc * su
        cconv = scw_ref[SC_CONV - 1:SC_CONV, :] * cu
        for k in range(SC_CONV - 1):
            cconv = cconv + scw_ref[k:k + 1, :] * sbuf_ref[:, k, :]
        for k in range(SC_CONV - 2):
            sbuf_out_ref[:, k, :] = sbuf_ref[:, k + 1, :]
        sbuf_out_ref[:, SC_CONV - 2, :] = cu
        c_out = sb * cconv
        g_out = g_out_ref[...]
        o_pad = o_scr[...]
        a_scale = lax.rsqrt(jnp.sum(o_pad * o_pad, axis=-1, keepdims=True) / D_ATTN + EPS)
        merged = jnp.concatenate([
            o_pad * a_scale * g_out[:, :D_QPAD],
            _rms(b_out, g_out[:, D_QPAD:D_QPAD + D_LRU]),
            _rms(c_out, g_out[:, D_QPAD + D_LRU:])], axis=-1).astype(_BF16)
        x1 = xs_scr[...] + jnp.dot(merged, w_out_ref[...], preferred_element_type=_F32)
        hb_scr[...] = _rms(x1, g_ffn_ref[...]).astype(_BF16)
        acc_scr[...] = x1

    @pl.when(s >= n_groups)
    def _():
        j = s - n_groups
        up = jnp.dot(hb_scr[...], w_up_ref[...], preferred_element_type=_F32)
        upc = fcw_ref[FFN_CONV - 1:FFN_CONV, :] * up
        for k in range(FFN_CONV - 1):
            upc = upc + fcw_ref[k:k + 1, :] * fst_ref[:, k, :]
        for k in range(FFN_CONV - 2):
            fst_out_ref[:, k, :] = fst_ref[:, k + 1, :]
        fst_out_ref[:, FFN_CONV - 2, :] = up

        @pl.when(j < N_FF_CHUNKS)
        def _():
            uc_scr[j] = upc

        @pl.when(j >= N_FF_CHUNKS)
        def _():
            act = (jax.nn.silu(upc) * uc_scr[j - N_FF_CHUNKS]).astype(_BF16)
            acc_scr[...] += jnp.dot(act, w_down_ref[...], preferred_element_type=_F32)

    @pl.when(s == last)
    def _():
        xs_scr[...] = acc_scr[...]

    @pl.when((s == last) & (l == DEPTH - 1))
    def _():
        y_ref[...] = _rms(acc_scr[...], g_final_ref[...])


def _sample_step(x, ck_t, cv_t, h0, lbuf, sbuf, fstate, sink_rows, g_mix, w_q, w_in, w_kvt, lcw,
                 lcb, w_gates, b_gates, lam, scw, g_out_pad, w_out_pad, g_ffn, w_up, fcw, w_down,
                 g_final):
    n_seq = x.shape[0]
    gs = SEQ_GROUP
    n_groups = n_seq // gs
    n_blocks = 2 * N_FF_CHUNKS
    steps = n_groups + n_blocks

    def per_layer(a):
        zeros = (0,) * (a.ndim - 1)
        return pl.BlockSpec((None,) + a.shape[1:], lambda l, s: (l,) + zeros)

    def group(l, s):
        return (l, jnp.minimum(s, n_groups - 1), 0, 0)

    def ff_block(l, s):
        return jnp.clip(s - n_groups, 0, n_blocks - 1)

    cache_spec = pl.BlockSpec((None, gs, D_KV, WINDOW), group)
    fst_spec = pl.BlockSpec((None, n_seq, FFN_CONV - 1, FF_CHUNK), lambda l, s: (l, 0, 0, ff_block(l, s)))
    in_specs = [
        pl.BlockSpec(x.shape, lambda l, s: (0, 0)),
        cache_spec, cache_spec, per_layer(h0), per_layer(lbuf), per_layer(sbuf), fst_spec,
        per_layer(sink_rows), per_layer(g_mix), per_layer(w_q), per_layer(w_in), per_layer(w_kvt),
        per_layer(lcw), per_layer(lcb), per_layer(w_gates), per_layer(b_gates), per_layer(lam),
        per_layer(scw), per_layer(g_out_pad), per_layer(w_out_pad), per_layer(g_ffn),
        pl.BlockSpec((None, D_MODEL, FF_CHUNK), lambda l, s: (l, 0, ff_block(l, s))),
        pl.BlockSpec((None, FFN_CONV, FF_CHUNK), lambda l, s: (l, 0, ff_block(l, s))),
        pl.BlockSpec((None, FF_CHUNK, D_MODEL),
                     lambda l, s: (l, jnp.clip(s - n_groups - N_FF_CHUNKS, 0, N_FF_CHUNKS - 1), 0)),
        pl.BlockSpec(g_final.shape, lambda l, s: (0, 0)),
    ]
    out_shape = (
        jax.ShapeDtypeStruct((n_seq, D_MODEL), _F32),
        jax.ShapeDtypeStruct(ck_t.shape, _F32),
        jax.ShapeDtypeStruct(cv_t.shape, _F32),
        jax.ShapeDtypeStruct(h0.shape, _F32),
        jax.ShapeDtypeStruct(lbuf.shape, _F32),
        jax.ShapeDtypeStruct(sbuf.shape, _F32),
        jax.ShapeDtypeStruct(fstate.shape, _F32),
    )
    out_specs = (
        pl.BlockSpec((n_seq, D_MODEL), lambda l, s: (0, 0)),
        cache_spec, cache_spec, per_layer(h0), per_layer(lbuf), per_layer(sbuf), fst_spec,
    )
    scratch = [
        pltpu.VMEM((n_seq, D_MODEL), _F32),
        pltpu.VMEM((n_seq, D_MODEL), _BF16),
        pltpu.VMEM((n_seq, D_MODEL), _F32),
        pltpu.VMEM((n_seq, D_QPAD), _F32),
        pltpu.VMEM((n_seq, D_IN - D_ATTN), _F32),
        pltpu.VMEM((n_seq, D_QPAD), _F32),
        pltpu.VMEM((N_FF_CHUNKS, n_seq, FF_CHUNK), _F32),
        pltpu.VMEM((n_seq, D_MODEL), _F32),
    ]
    return pl.pallas_call(
        functools.partial(_sample_kernel, n_groups=n_groups),
        grid=(DEPTH, steps),
        in_specs=in_specs,
        out_specs=out_specs,
        out_shape=out_shape,
        scratch_shapes=scratch,
        compiler_params=pltpu.CompilerParams(
            dimension_semantics=("arbitrary", "arbitrary"),
            vmem_limit_bytes=VMEM_LIMIT_BYTES),
        name="sample_step",
    )(x, ck_t, cv_t, h0, lbuf, sbuf, fstate, sink_rows, g_mix, w_q, w_in, w_kvt, lcw, lcb, w_gates,
      b_gates, lam, scw, g_out_pad, w_out_pad, g_ffn, w_up, fcw, w_down, g_final)


def _block_diag(w):
    eye = jnp.eye(N_LRU_BLOCKS, dtype=w.dtype).reshape(1, N_LRU_BLOCKS, 1, N_LRU_BLOCKS, 1)
    return (w[:, :, :, None, :] * eye).reshape(DEPTH, D_LRU, D_LRU)


def _pad_heads(w, axis):
    w = jnp.moveaxis(w, axis, -1)
    lead = w.shape[:-1]
    w = w.reshape(lead + (N_KV_HEADS, GQA_GROUP, 1, HEAD_DIM))
    eye = jnp.eye(N_KV_HEADS, dtype=w.dtype).reshape(N_KV_HEADS, 1, N_KV_HEADS, 1)
    w = (w * eye).reshape(lead + (D_QPAD,))
    return jnp.moveaxis(w, -1, axis)


def kernel(x_prompt, x_sample, cache_k, cache_v, state_rglru, state_lru_conv, state_sconv, state_ffn_conv, g_mix, w_in, sinks, lru_conv_w, lru_conv_b, lru_wa, lru_ba, lru_wi, lru_bi, lru_lambda, sc_conv_w, g_out, w_out, g_ffn, w_up, ffn_conv_w, w_down, g_final):
    batch = x_prompt.shape[0]
    n_seq = x_sample.shape[0]

    w_in_b = w_in.astype(_BF16)
    w_out_b = w_out.astype(_BF16)
    w_up_b = w_up.astype(_BF16)
    w_down_b = w_down.astype(_BF16)
    w_gates = jnp.concatenate([_block_diag(lru_wa), _block_diag(lru_wi)], axis=-1).astype(_BF16)
    b_gates = jnp.concatenate([lru_ba, lru_bi], axis=-1)[:, None, :]
    w_q_pad = _pad_heads(w_in[:, :, :D_ATTN], 2).astype(_BF16)
    w_kv = lax.optimization_barrier(w_in[:, :, _K0:_K0 + 2 * D_KV].astype(_BF16))
    w_kvt = jnp.swapaxes(w_kv, 1, 2)
    w_out_pad = jnp.concatenate([_pad_heads(w_out[:, :D_ATTN], 1), w_out[:, D_ATTN:]],
                                axis=1).astype(_BF16)
    g_out_pad = jnp.concatenate([_pad_heads(g_out[:, :D_ATTN], 1), g_out[:, D_ATTN:]], axis=1)
    sink_rows = jnp.broadcast_to(
        jnp.repeat(sinks, SEQ_GROUP, axis=1)[:, :, None], (DEPTH, N_HEADS * SEQ_GROUP, D_KV))

    row = lambda a: a[:, None, :]
    g_mix_r, g_out_r, g_ffn_r = row(g_mix), row(g_out), row(g_ffn)
    lcb_r, lam_r, g_out_pad_r = row(lru_conv_b), row(lru_lambda), row(g_out_pad)
    g_final_r = g_final[None, :]

    ck_t = jnp.transpose(cache_k, (0, 1, 3, 4, 2)).reshape(DEPTH, n_seq, D_KV, WINDOW)
    cv_t = jnp.transpose(cache_v, (0, 1, 3, 4, 2)).reshape(DEPTH, n_seq, D_KV, WINDOW)
    lbuf_t = jnp.transpose(state_lru_conv, (0, 2, 1, 3))
    ys, ck_new, cv_new, h_new, lbuf_new, sbuf_new, fst_new = _sample_step(
        x_sample.reshape(n_seq, D_MODEL), ck_t, cv_t, state_rglru, lbuf_t, state_sconv,
        state_ffn_conv, sink_rows, g_mix_r, w_q_pad, w_in_b, w_kvt, lru_conv_w, lcb_r, w_gates,
        b_gates, lam_r, sc_conv_w, g_out_pad_r, w_out_pad, g_ffn_r, w_up_b, ffn_conv_w, w_down_b,
        g_final_r)
    unview = lambda c: jnp.transpose(
        c.reshape(DEPTH, n_seq, N_KV_HEADS, HEAD_DIM, WINDOW), (0, 1, 4, 2, 3))
    s_out = (unview(ck_new), unview(cv_new), h_new, jnp.transpose(lbuf_new, (0, 2, 1, 3)),
             sbuf_new, fst_new)

    xp = x_prompt
    p_states = []
    for l in range(DEPTH):
        xp, kv_new, h_last, lbuf_p, sbuf_p, fbuf_p = _prompt_layer(
            l, xp, sinks, g_mix_r, w_in_b, lru_conv_w, lcb_r, w_gates, b_gates, lam_r, sc_conv_w,
            g_out_r, w_out_b, g_ffn_r, w_up_b, ffn_conv_w, w_down_b, g_final_r,
            final=(l == DEPTH - 1))
        p_states.append((
            kv_new[:, :, :D_KV].reshape(batch, WINDOW, N_KV_HEADS, HEAD_DIM),
            kv_new[:, :, D_KV:].reshape(batch, WINDOW, N_KV_HEADS, HEAD_DIM),
            h_last.reshape(batch, D_LRU), lbuf_p, sbuf_p, fbuf_p))

    stack = lambda i: jnp.stack([st[i] for st in p_states])
    return ((xp, ys.reshape(n_seq, 1, D_MODEL)) + tuple(stack(i) for i in range(6)) + s_out)
```

```python
import functools

import jax
import jax.numpy as jnp
from jax import lax
from jax.experimental import pallas as pl
from jax.experimental.pallas import tpu as pltpu

D_MODEL = 1024
DEPTH = 4
HEAD_DIM = 64
N_HEADS = 8
N_KV_HEADS = 2
GQA_GROUP = N_HEADS // N_KV_HEADS
WINDOW = 128
D_ATTN = N_HEADS * HEAD_DIM
D_KV = N_KV_HEADS * HEAD_DIM
D_LRU = 256
N_LRU_BLOCKS = 4
LRU_BLOCK = D_LRU // N_LRU_BLOCKS
LRU_CONV = 4
LRU_C = 8.0
D_SC = 256
SC_CONV = 3
D_MIX = D_ATTN + D_LRU + D_SC
D_IN = D_ATTN + 2 * D_KV + 2 * D_LRU + 3 * D_SC
D_FF = 2816
FFN_CONV = 3
EPS = 1e-6

_Q0, _K0, _V0 = 0, D_ATTN, D_ATTN + D_KV
_LX0 = D_ATTN + 2 * D_KV
_LG0 = _LX0 + D_LRU
_SB0 = _LG0 + D_LRU
_SC0 = _SB0 + D_SC
_SU0 = _SC0 + D_SC

SUBLANES = 8
LANES = 128
TOKEN_BLOCK = 512
FF_CHUNK = 256
N_FF_CHUNKS = D_FF // FF_CHUNK
SEQ_GROUP = 8
D_QPAD = N_HEADS * D_KV
VMEM_LIMIT_BYTES = 56 * 1024 * 1024

_F32 = jnp.float32
_BF16 = jnp.bfloat16
_NT = (((1,), (1,)), ((), ()))

def _rms(x, g):
    ms = jnp.mean(x * x, axis=-1, keepdims=True)
    return x * lax.rsqrt(ms + EPS) * g


def _softplus(x):
    return jnp.maximum(x, 0.0) + jnp.log1p(jnp.exp(-jnp.abs(x)))


def _lru_coeffs(xc, gates, lam):
    r = jax.nn.sigmoid(gates[:, :D_LRU])
    i = jax.nn.sigmoid(gates[:, D_LRU:])
    log_a = (-LRU_C) * r * _softplus(-lam)
    a = jnp.exp(log_a)
    b = jnp.sqrt(jnp.tanh(-log_a) * (1.0 + a * a)) * (i * xc)
    return a, b


def _scan_rows(a, b, scr):
    n, c = a.shape
    a_t = [a[:, j * LANES:(j + 1) * LANES] for j in range(c // LANES)]
    b_t = [b[:, j * LANES:(j + 1) * LANES] for j in range(c // LANES)]
    d = 1
    while d < SUBLANES:
        for j in range(c // LANES):
            scr[0, j, SUBLANES:SUBLANES + n, :] = a_t[j]
            scr[1, j, SUBLANES:SUBLANES + n, :] = b_t[j]
            a_prev = scr[0, j, SUBLANES - d:SUBLANES - d + n, :]
            b_prev = scr[1, j, SUBLANES - d:SUBLANES - d + n, :]
            b_t[j] = a_t[j] * b_prev + b_t[j]
            a_t[j] = a_t[j] * a_prev
        d *= 2
    a = jnp.concatenate(a_t, axis=-1)
    b = jnp.concatenate(b_t, axis=-1)
    while d < n:
        b = jnp.concatenate([b[:d], a[d:] * b[:-d] + b[d:]], axis=0)
        a = jnp.concatenate([a[:d], a[d:] * a[:-d]], axis=0)
        d *= 2
    return a, b


def _causal_conv(u, w_ref, col0, scr, hist=None):
    n, c = u.shape
    taps = w_ref.shape[0]
    h0 = SUBLANES - (taps - 1)
    outs, tails = [], []
    for j in range(c // LANES):
        wcols = slice(col0 + j * LANES, col0 + (j + 1) * LANES)
        uj = u[:, j * LANES:(j + 1) * LANES]
        scr[j, SUBLANES:SUBLANES + n, :] = uj
        if hist is not None:
            scr[j, h0:SUBLANES, :] = hist[j, h0:SUBLANES, :]
        y = w_ref[taps - 1:taps, wcols] * uj
        for k in range(taps - 1):
            y = y + w_ref[k:k + 1, wcols] * scr[j, h0 + k:h0 + k + n, :]
        tail = scr[j, n + h0:n + SUBLANES, :]
        (scr if hist is None else hist)[j, h0:SUBLANES, :] = tail
        outs.append(y)
        tails.append(tail)
    return jnp.concatenate(outs, axis=-1), jnp.concatenate(tails, axis=-1)


def _prompt_layer_kernel(
        sinks_ref, x_ref, g_mix_ref, w_in_ref, lcw_ref, lcb_ref, w_gates_ref, b_gates_ref,
        lam_ref, scw_ref, g_out_ref, w_out_ref, g_ffn_ref, w_up_ref, fcw_ref, w_down_ref,
        g_final_ref,
        y_ref, y_last_ref, kv_out_ref, h_out_ref, lbuf_out_ref, sbuf_out_ref, fbuf_out_ref,
        kv_scr, lc_scr, sc_scr, h_scr, scan_scr, up_scr, fhist_scr, x1_scr, hn_scr, act_scr,
        *, layer, final, blocks_per_seq, n_blocks):
    tb = TOKEN_BLOCK
    i = pl.program_id(0)
    t = lax.rem(i, blocks_per_seq)

    def finish_cols(cols):
        return x1_scr[:, cols] + jnp.dot(act_scr[...], w_down_ref[:, cols],
                                         preferred_element_type=_F32)

    @pl.when(i == 0)
    def _():
        x1_scr[...] = jnp.zeros_like(x1_scr)
        act_scr[...] = jnp.zeros_like(act_scr)
        ident = (tb // WINDOW, 1, D_LRU // LANES, SUBLANES, LANES)
        scan_scr[:, 0:1, :, 0:SUBLANES, :] = jnp.ones(ident, _F32)
        scan_scr[:, 1:2, :, 0:SUBLANES, :] = jnp.zeros(ident, _F32)

    @pl.when(t == 0)
    def _():
        kv_scr[0:WINDOW, :] = jnp.zeros((WINDOW, 2 * D_KV), _BF16)
        lc_scr[:, 0:SUBLANES, :] = jnp.zeros((D_LRU // LANES, SUBLANES, LANES), _F32)
        sc_scr[:, 0:SUBLANES, :] = jnp.zeros((D_SC // LANES, SUBLANES, LANES), _F32)
        h_scr[...] = jnp.zeros_like(h_scr)
        fhist_scr[...] = jnp.zeros_like(fhist_scr)

    @pl.when(t > 0)
    def _():
        kv_scr[0:WINDOW, :] = kv_scr[tb:tb + WINDOW, :]

    x = x_ref[0]
    hb = _rms(x, g_mix_ref[...]).astype(_BF16)

    zf = jnp.dot(hb, w_in_ref[...], preferred_element_type=_F32)
    z = {c0: zf[:, c0:c0 + D_LRU] for c0 in (_LX0, _LG0, _SB0, _SC0, _SU0)}

    q = (zf[:, _Q0:_Q0 + D_ATTN] * (HEAD_DIM ** -0.5)).astype(_BF16)
    kv_scr[WINDOW:WINDOW + tb, :] = zf[:, _K0:_K0 + 2 * D_KV].astype(_BF16)
    kv_out_ref[0] = zf[tb - WINDOW:tb, _K0:_K0 + 2 * D_KV]

    qi = lax.broadcasted_iota(jnp.int32, (WINDOW, 2 * WINDOW), 0)
    kj = lax.broadcasted_iota(jnp.int32, (WINDOW, 2 * WINDOW), 1)
    band = (kj >= qi) & (kj <= qi + WINDOW)
    band_first = band & (kj >= WINDOW * (1 - jnp.minimum(t, 1)))

    a_blocks = []
    n_sub = tb // WINDOW
    for j in range(n_sub):
        keys = kv_scr[j * WINDOW:(j + 2) * WINDOW, :]
        qj = q[j * WINDOW:(j + 1) * WINDOW, :]
        mask = band_first if j == 0 else band
        outs = []
        for h in range(N_HEADS):
            g = h // GQA_GROUP
            sink = sinks_ref[layer, h]
            s = lax.dot_general(qj[:, h * HEAD_DIM:(h + 1) * HEAD_DIM],
                                keys[:, g * HEAD_DIM:(g + 1) * HEAD_DIM], _NT,
                                preferred_element_type=_F32)
            s = jnp.where(mask, s, -jnp.inf)
            m = jnp.maximum(jnp.max(s, axis=-1, keepdims=True), sink)
            p = jnp.exp(s - m)
            denom = jnp.sum(p, axis=-1, keepdims=True) + jnp.exp(sink - m)
            o = jnp.dot(p.astype(_BF16),
                        keys[:, D_KV + g * HEAD_DIM:D_KV + (g + 1) * HEAD_DIM],
                        preferred_element_type=_F32)
            outs.append(o / denom)
        a_blocks.append(jnp.concatenate(outs, axis=-1))
    a_out = jnp.concatenate(a_blocks, axis=0)

    xc, tail = _causal_conv(z[_LX0], lcw_ref, 0, lc_scr)
    xc = xc + lcb_ref[...]
    lbuf_out_ref[0] = tail

    gates = jnp.dot(xc.astype(_BF16), w_gates_ref[...],
                    preferred_element_type=_F32) + b_gates_ref[...]
    a_dec, b_in = _lru_coeffs(xc, gates, lam_ref[...])
    carry = h_scr[0:1, :]
    h_blocks = []
    for j in range(n_sub):
        rows = slice(j * WINDOW, (j + 1) * WINDOW)
        a_cum, b_cum = _scan_rows(a_dec[rows], b_in[rows], scan_scr.at[j])
        hj = a_cum * carry + b_cum
        carry = hj[WINDOW - 1:WINDOW, :]
        h_blocks.append(hj)
        cols = slice(j * (D_MODEL // n_sub), (j + 1) * (D_MODEL // n_sub))
        y_ref[0, :, cols] = finish_cols(cols)
    if final:
        y_ref[0] = _rms(y_ref[0], g_final_ref[...])
    hseq = jnp.concatenate(h_blocks, axis=0)
    h_scr[0:1, :] = carry
    h_out_ref[0] = carry
    b_out = hseq * jax.nn.gelu(z[_LG0])

    cu = z[_SC0] * z[_SU0]
    cconv, tail = _causal_conv(cu, scw_ref, 0, sc_scr)
    sbuf_out_ref[0] = tail
    c_out = z[_SB0] * cconv

    g_out = g_out_ref[...]
    merged = jnp.concatenate([
        _rms(a_out, g_out[:, :D_ATTN]),
        _rms(b_out, g_out[:, D_ATTN:D_ATTN + D_LRU]),
        _rms(c_out, g_out[:, D_ATTN + D_LRU:])], axis=-1).astype(_BF16)
    x1 = x + jnp.dot(merged, w_out_ref[...], preferred_element_type=_F32)

    hn_scr[...] = _rms(x1, g_ffn_ref[...]).astype(_BF16)
    x1_scr[...] = x1
    n_t = FF_CHUNK // LANES
    for c in range(N_FF_CHUNKS):
        cu0, cg0 = c * FF_CHUNK, D_FF + c * FF_CHUNK
        slot = c % 2
        up_u = jnp.dot(hn_scr[...], w_up_ref[:, cu0:cu0 + FF_CHUNK], preferred_element_type=_F32)
        up_g = jnp.dot(hn_scr[...], w_up_ref[:, cg0:cg0 + FF_CHUNK], preferred_element_type=_F32)
        uc, tail_u = _causal_conv(up_u, fcw_ref, cu0, up_scr.at[slot, 0:n_t],
                                  hist=fhist_scr.at[c, 0:n_t])
        gc, tail_g = _causal_conv(up_g, fcw_ref, cg0, up_scr.at[slot, n_t:2 * n_t],
                                  hist=fhist_scr.at[c, n_t:2 * n_t])
        fbuf_out_ref[0, :, cu0:cu0 + FF_CHUNK] = tail_u
        fbuf_out_ref[0, :, cg0:cg0 + FF_CHUNK] = tail_g
        act_scr[:, cu0:cu0 + FF_CHUNK] = (jax.nn.silu(gc) * uc).astype(_BF16)

    @pl.when(i == n_blocks - 1)
    def _():
        x2 = finish_cols(slice(0, D_MODEL))
        y_last_ref[...] = _rms(x2, g_final_ref[...]) if final else x2


def _layer_spec(a, layer):
    zeros = (0,) * (a.ndim - 1)
    return pl.BlockSpec((None,) + a.shape[1:], lambda *_: (layer,) + zeros,
                        pipeline_mode=pl.Buffered(1))


def _prompt_layer(layer, x, sinks, g_mix, w_in, lcw, lcb, w_gates, b_gates, lam, scw, g_out, w_out,
                  g_ffn, w_up, fcw, w_down, g_final, *, final):
    batch, seq, _ = x.shape
    tb = TOKEN_BLOCK
    stacked = (g_mix, w_in, lcw, lcb, w_gates, b_gates, lam, scw, g_out, w_out, g_ffn, w_up,
               fcw, w_down)
    bps = seq // tb
    n_blocks = batch * bps

    def cur(i):
        return (i // bps, lax.rem(i, bps), 0)

    def prev(i):
        return cur(jnp.maximum(i - 1, 0))

    def seq_of(i):
        return (i // bps, 0, 0)

    in_specs = [pl.BlockSpec(memory_space=pltpu.SMEM),
                pl.BlockSpec((1, tb, D_MODEL), cur)]
    in_specs += [_layer_spec(a, layer) for a in stacked]
    in_specs += [pl.BlockSpec(g_final.shape, lambda i: (0, 0), pipeline_mode=pl.Buffered(1))]
    out_shape = (
        jax.ShapeDtypeStruct((batch, seq, D_MODEL), _F32),
        jax.ShapeDtypeStruct((tb, D_MODEL), _F32),
        jax.ShapeDtypeStruct((batch, WINDOW, 2 * D_KV), _F32),
        jax.ShapeDtypeStruct((batch, 1, D_LRU), _F32),
        jax.ShapeDtypeStruct((batch, LRU_CONV - 1, D_LRU), _F32),
        jax.ShapeDtypeStruct((batch, SC_CONV - 1, D_SC), _F32),
        jax.ShapeDtypeStruct((batch, FFN_CONV - 1, 2 * D_FF), _F32),
    )
    out_specs = (
        pl.BlockSpec((1, tb, D_MODEL), prev),
        pl.BlockSpec((tb, D_MODEL), lambda i: (0, 0)),
        pl.BlockSpec((1, WINDOW, 2 * D_KV), seq_of),
        pl.BlockSpec((1, 1, D_LRU), seq_of),
        pl.BlockSpec((1, LRU_CONV - 1, D_LRU), seq_of),
        pl.BlockSpec((1, SC_CONV - 1, D_SC), seq_of),
        pl.BlockSpec((1, FFN_CONV - 1, 2 * D_FF), seq_of),
    )
    n_t = 2 * FF_CHUNK // LANES
    scratch = [
        pltpu.VMEM((WINDOW + tb, 2 * D_KV), _BF16),
        pltpu.VMEM((D_LRU // LANES, SUBLANES + tb, LANES), _F32),
        pltpu.VMEM((D_SC // LANES, SUBLANES + tb, LANES), _F32),
        pltpu.VMEM((SUBLANES, D_LRU), _F32),
        pltpu.VMEM((tb // WINDOW, 2, D_LRU // LANES, SUBLANES + WINDOW, LANES), _F32),
        pltpu.VMEM((2, n_t, SUBLANES + tb, LANES), _F32),
        pltpu.VMEM((N_FF_CHUNKS, n_t, SUBLANES, LANES), _F32),
        pltpu.VMEM((tb, D_MODEL), _F32),
        pltpu.VMEM((tb, D_MODEL), _BF16),
        pltpu.VMEM((tb, D_FF), _BF16),
    ]
    y, y_last, *states = pl.pallas_call(
        functools.partial(_prompt_layer_kernel, layer=layer, final=final, blocks_per_seq=bps,
                          n_blocks=n_blocks),
        grid=(n_blocks,),
        in_specs=in_specs,
        out_specs=out_specs,
        out_shape=out_shape,
        scratch_shapes=scratch,
        compiler_params=pltpu.CompilerParams(
            dimension_semantics=("arbitrary",),
            vmem_limit_bytes=VMEM_LIMIT_BYTES),
        name="prompt_layer",
    )(sinks, x, *stacked, g_final)
    y = lax.dynamic_update_slice(y, y_last[None], (batch - 1, seq - tb, 0))
    return (y, *states)


def _sample_kernel(
        x_ref, ck_ref, cv_ref, h0_ref, lbuf_ref, sbuf_ref, fst_ref, sink_rows_ref,
        g_mix_ref, w_q_ref, w_in_ref, w_kvt_ref, lcw_ref, lcb_ref, w_gates_ref, b_gates_ref,
        lam_ref, scw_ref, g_out_ref, w_out_ref, g_ffn_ref, w_up_ref, fcw_ref, w_down_ref,
        g_final_ref,
        y_ref, ck_out_ref, cv_out_ref, h_out_ref, lbuf_out_ref, sbuf_out_ref, fst_out_ref,
        xs_scr, hb_scr, hbf_scr, q_scr, rest_scr, o_scr, uc_scr, acc_scr,
        *, n_groups):
    l = pl.program_id(0)
    s = pl.program_id(1)
    gs = SEQ_GROUP
    n_blocks = 2 * N_FF_CHUNKS
    last = n_groups + n_blocks - 1

    @pl.when((l == 0) & (s == 0))
    def _():
        xs_scr[...] = x_ref[...]

    @pl.when(s == 0)
    def _():
        hb = _rms(xs_scr[...], g_mix_ref[...]).astype(_BF16)
        hb_scr[...] = hb
        hbf_scr[...] = hb.astype(_F32)
        q_scr[...] = jnp.dot(hb, w_q_ref[...], preferred_element_type=_F32) * (HEAD_DIM ** -0.5)
        rest_scr[...] = jnp.dot(hb, w_in_ref[:, D_ATTN:], preferred_element_type=_F32)

    @pl.when(s < n_groups)
    def _():
        row0 = pl.multiple_of(s * gs, gs)
        q_all = jnp.concatenate(
            [q_scr[pl.ds(row0, gs), h * D_KV:(h + 1) * D_KV] for h in range(N_HEADS)], axis=0)
        k_new = rest_scr[pl.ds(row0, gs), 0:D_KV]
        v_new = rest_scr[pl.ds(row0, gs), D_KV:2 * D_KV]
        k_new_rows = jnp.concatenate([k_new] * N_HEADS, axis=0)
        v_new_rows = jnp.concatenate([v_new] * N_HEADS, axis=0)

        keys_t = jnp.concatenate([ck_ref[b] for b in range(gs)], axis=-1).astype(_BF16)
        s_all = jnp.dot(q_all.astype(_BF16), keys_t, preferred_element_type=_F32)
        seq_of_row = lax.broadcasted_iota(jnp.int32, (N_HEADS * gs, WINDOW), 0) & (gs - 1)
        sc = jnp.zeros((N_HEADS * gs, WINDOW), _F32)
        for j in range(gs):
            sc = sc + jnp.where(seq_of_row == j, s_all[:, j * WINDOW:(j + 1) * WINDOW], 0.0)
        s_new = jnp.sum(q_all * k_new_rows, axis=-1, keepdims=True)
        sink = sink_rows_ref[:, 0:1]
        m = jnp.maximum(jnp.maximum(jnp.max(sc, axis=-1, keepdims=True), s_new), sink)
        p = jnp.exp(sc - m)
        p_new = jnp.exp(s_new - m)
        denom = jnp.sum(p, axis=-1, keepdims=True) + p_new + jnp.exp(sink - m)
        p_blk = jnp.concatenate(
            [jnp.where(seq_of_row == j, p, 0.0) for j in range(gs)], axis=-1).astype(_BF16)
        vals_t = jnp.concatenate([cv_ref[b] for b in range(gs)], axis=-1).astype(_BF16)
        o = lax.dot_general(p_blk, vals_t, _NT, preferred_element_type=_F32) + p_new * v_new_rows
        o = o / denom
        row = lax.broadcasted_iota(jnp.int32, (N_HEADS * gs, D_KV), 0)
        lane = lax.broadcasted_iota(jnp.int32, (N_HEADS * gs, D_KV), 1)
        own_kv = (lax.shift_right_logical(row, (GQA_GROUP * gs).bit_length() - 1)
                  == lax.shift_right_logical(lane, HEAD_DIM.bit_length() - 1))
        o = jnp.where(own_kv, o, 0.0)
        for h in range(N_HEADS):
            o_scr[pl.ds(row0, gs), h * D_KV:(h + 1) * D_KV] = o[h * gs:(h + 1) * gs, :]

        hb_g = hbf_scr[pl.ds(row0, gs), :].astype(_BF16)
        kv_new_t = lax.dot_general(w_kvt_ref[...], hb_g, _NT, preferred_element_type=_F32)
        pos = lax.broadcasted_iota(jnp.int32, (D_KV, WINDOW), 1)
        for b in range(gs):
            ck_out_ref[b] = jnp.where(pos == WINDOW - 1, kv_new_t[0:D_KV, b:b + 1],
                                      pltpu.roll(ck_ref[b], WINDOW - 1, 1))
            cv_out_ref[b] = jnp.where(pos == WINDOW - 1, kv_new_t[D_KV:2 * D_KV, b:b + 1],
                                      pltpu.roll(cv_ref[b], WINDOW - 1, 1))

    @pl.when(s == n_groups - 1)
    def _():
        rest = rest_scr[...]
        base = 2 * D_KV
        lx = rest[:, base:base + D_LRU]
        lg = rest[:, base + D_LRU:base + 2 * D_LRU]
        sb = rest[:, base + 2 * D_LRU:base + 2 * D_LRU + D_SC]
        scc = rest[:, base + 2 * D_LRU + D_SC:base + 2 * D_LRU + 2 * D_SC]
        su = rest[:, base + 2 * D_LRU + 2 * D_SC:]
        xc = lcb_ref[...] + lcw_ref[LRU_CONV - 1:LRU_CONV, :] * lx
        for k in range(LRU_CONV - 1):
            xc = xc + lcw_ref[k:k + 1, :] * lbuf_ref[k]
        for k in range(LRU_CONV - 2):
            lbuf_out_ref[k] = lbuf_ref[k + 1]
        lbuf_out_ref[LRU_CONV - 2] = lx
        gates = jnp.dot(xc.astype(_BF16), w_gates_ref[...],
                        preferred_element_type=_F32) + b_gates_ref[...]
        a_dec, b_in = _lru_coeffs(xc, gates, lam_ref[...])
        h_new = a_dec * h0_ref[...] + b_in
        h_out_ref[...] = h_new
        b_out = h_new * jax.nn.gelu(lg)
        cu = scc * su
        cconv = scw_ref[SC_CONV - 1:SC_CONV, :] * cu
        for k in range(SC_CONV - 1):
            cconv = cconv + scw_ref[k:k + 1, :] * sbuf_ref[:, k, :]
        for k in range(SC_CONV - 2):
            sbuf_out_ref[:, k, :] = sbuf_ref[:, k + 1, :]
        sbuf_out_ref[:, SC_CONV - 2, :] = cu
        c_out = sb * cconv
        g_out = g_out_ref[...]
        o_pad = o_scr[...]
        a_scale = lax.rsqrt(jnp.sum(o_pad * o_pad, axis=-1, keepdims=True) / D_ATTN + EPS)
        merged = jnp.concatenate([
            o_pad * a_scale * g_out[:, :D_QPAD],
            _rms(b_out, g_out[:, D_QPAD:D_QPAD + D_LRU]),
            _rms(c_out, g_out[:, D_QPAD + D_LRU:])], axis=-1).astype(_BF16)
        x1 = xs_scr[...] + jnp.dot(merged, w_out_ref[...], preferred_element_type=_F32)
        hb_scr[...] = _rms(x1, g_ffn_ref[...]).astype(_BF16)
        acc_scr[...] = x1

    @pl.when(s >= n_groups)
    def _():
        j = s - n_groups
        up = jnp.dot(hb_scr[...], w_up_ref[...], preferred_element_type=_F32)
        upc = fcw_ref[FFN_CONV - 1:FFN_CONV, :] * up
        for k in range(FFN_CONV - 1):
            upc = upc + fcw_ref[k:k + 1, :] * fst_ref[:, k, :]
        for k in range(FFN_CONV - 2):
            fst_out_ref[:, k, :] = fst_ref[:, k + 1, :]
        fst_out_ref[:, FFN_CONV - 2, :] = up

        @pl.when(j < N_FF_CHUNKS)
        def _():
            uc_scr[j] = upc

        @pl.when(j >= N_FF_CHUNKS)
        def _():
            act = (jax.nn.silu(upc) * uc_scr[j - N_FF_CHUNKS]).astype(_BF16)
            acc_scr[...] += jnp.dot(act, w_down_ref[...], preferred_element_type=_F32)

    @pl.when(s == last)
    def _():
        xs_scr[...] = acc_scr[...]

    @pl.when((s == last) & (l == DEPTH - 1))
    def _():
        y_ref[...] = _rms(acc_scr[...], g_final_ref[...])


def _sample_step(x, ck_t, cv_t, h0, lbuf, sbuf, fstate, sink_rows, g_mix, w_q, w_in, w_kvt, lcw,
                 lcb, w_gates, b_gates, lam, scw, g_out_pad, w_out_pad, g_ffn, w_up, fcw, w_down,
                 g_final):
    n_seq = x.shape[0]
    gs = SEQ_GROUP
    n_groups = n_seq // gs
    n_blocks = 2 * N_FF_CHUNKS
    steps = n_groups + n_blocks

    def per_layer(a):
        zeros = (0,) * (a.ndim - 1)
        return pl.BlockSpec((None,) + a.shape[1:], lambda l, s: (l,) + zeros)

    def group(l, s):
        return (l, jnp.minimum(s, n_groups - 1), 0, 0)

    def ff_block(l, s):
        return jnp.clip(s - n_groups, 0, n_blocks - 1)

    cache_spec = pl.BlockSpec((None, gs, D_KV, WINDOW), group)
    fst_spec = pl.BlockSpec((None, n_seq, FFN_CONV - 1, FF_CHUNK), lambda l, s: (l, 0, 0, ff_block(l, s)))
    in_specs = [
        pl.BlockSpec(x.shape, lambda l, s: (0, 0)),
        cache_spec, cache_spec, per_layer(h0), per_layer(lbuf), per_layer(sbuf), fst_spec,
        per_layer(sink_rows), per_layer(g_mix), per_layer(w_q), per_layer(w_in), per_layer(w_kvt),
        per_layer(lcw), per_layer(lcb), per_layer(w_gates), per_layer(b_gates), per_layer(lam),
        per_layer(scw), per_layer(g_out_pad), per_layer(w_out_pad), per_layer(g_ffn),
        pl.BlockSpec((None, D_MODEL, FF_CHUNK), lambda l, s: (l, 0, ff_block(l, s))),
        pl.BlockSpec((None, FFN_CONV, FF_CHUNK), lambda l, s: (l, 0, ff_block(l, s))),
        pl.BlockSpec((None, FF_CHUNK, D_MODEL),
                     lambda l, s: (l, jnp.clip(s - n_groups - N_FF_CHUNKS, 0, N_FF_CHUNKS - 1), 0)),
        pl.BlockSpec(g_final.shape, lambda l, s: (0, 0)),
    ]
    out_shape = (
        jax.ShapeDtypeStruct((n_seq, D_MODEL), _F32),
        jax.ShapeDtypeStruct(ck_t.shape, _F32),
        jax.ShapeDtypeStruct(cv_t.shape, _F32),
        jax.ShapeDtypeStruct(h0.shape, _F32),
        jax.ShapeDtypeStruct(lbuf.shape, _F32),
        jax.ShapeDtypeStruct(sbuf.shape, _F32),
        jax.ShapeDtypeStruct(fstate.shape, _F32),
    )
    out_specs = (
        pl.BlockSpec((n_seq, D_MODEL), lambda l, s: (0, 0)),
        cache_spec, cache_spec, per_layer(h0), per_layer(lbuf), per_layer(sbuf), fst_spec,
    )
    scratch = [
        pltpu.VMEM((n_seq, D_MODEL), _F32),
        pltpu.VMEM((n_seq, D_MODEL), _BF16),
        pltpu.VMEM((n_seq, D_MODEL), _F32),
        pltpu.VMEM((n_seq, D_QPAD), _F32),
        pltpu.VMEM((n_seq, D_IN - D_ATTN), _F32),
        pltpu.VMEM((n_seq, D_QPAD), _F32),
        pltpu.VMEM((N_FF_CHUNKS, n_seq, FF_CHUNK), _F32),
        pltpu.VMEM((n_seq, D_MODEL), _F32),
    ]
    return pl.pallas_call(
        functools.partial(_sample_kernel, n_groups=n_groups),
        grid=(DEPTH, steps),
        in_specs=in_specs,
        out_specs=out_specs,
        out_shape=out_shape,
        scratch_shapes=scratch,
        compiler_params=pltpu.CompilerParams(
            dimension_semantics=("arbitrary", "arbitrary"),
            vmem_limit_bytes=VMEM_LIMIT_BYTES),
        name="sample_step",
    )(x, ck_t, cv_t, h0, lbuf, sbuf, fstate, sink_rows, g_mix, w_q, w_in, w_kvt, lcw, lcb, w_gates,
      b_gates, lam, scw, g_out_pad, w_out_pad, g_ffn, w_up, fcw, w_down, g_final)


def _block_diag(w):
    eye = jnp.eye(N_LRU_BLOCKS, dtype=w.dtype).reshape(1, N_LRU_BLOCKS, 1, N_LRU_BLOCKS, 1)
    return (w[:, :, :, None, :] * eye).reshape(DEPTH, D_LRU, D_LRU)


def _pad_heads(w, axis):
    pieces = []
    for h in range(N_HEADS):
        blk = lax.slice_in_dim(w, h * HEAD_DIM, (h + 1) * HEAD_DIM, axis=axis)
        for g in range(N_KV_HEADS):
            pieces.append(blk if g == h // GQA_GROUP else jnp.zeros_like(blk))
    return jnp.concatenate(pieces, axis=axis)


def kernel(x_prompt, x_sample, cache_k, cache_v, state_rglru, state_lru_conv, state_sconv, state_ffn_conv, g_mix, w_in, sinks, lru_conv_w, lru_conv_b, lru_wa, lru_ba, lru_wi, lru_bi, lru_lambda, sc_conv_w, g_out, w_out, g_ffn, w_up, ffn_conv_w, w_down, g_final):
    batch = x_prompt.shape[0]
    n_seq = x_sample.shape[0]

    w_in_b = w_in.astype(_BF16)
    w_out_b = w_out.astype(_BF16)
    w_up_b = w_up.astype(_BF16)
    w_down_b = w_down.astype(_BF16)
    w_gates = jnp.concatenate([_block_diag(lru_wa), _block_diag(lru_wi)], axis=-1).astype(_BF16)
    b_gates = jnp.concatenate([lru_ba, lru_bi], axis=-1)[:, None, :]
    w_q_pad = _pad_heads(w_in[:, :, :D_ATTN], 2).astype(_BF16)
    w_kv = lax.optimization_barrier(w_in[:, :, _K0:_K0 + 2 * D_KV].astype(_BF16))
    w_kvt = jnp.swapaxes(w_kv, 1, 2)
    w_out_pad = jnp.concatenate([_pad_heads(w_out[:, :D_ATTN], 1), w_out[:, D_ATTN:]],
                                axis=1).astype(_BF16)
    g_out_pad = jnp.concatenate([_pad_heads(g_out[:, :D_ATTN], 1), g_out[:, D_ATTN:]], axis=1)
    sink_rows = jnp.broadcast_to(
        jnp.repeat(sinks, SEQ_GROUP, axis=1)[:, :, None], (DEPTH, N_HEADS * SEQ_GROUP, D_KV))

    row = lambda a: a[:, None, :]
    g_mix_r, g_out_r, g_ffn_r = row(g_mix), row(g_out), row(g_ffn)
    lcb_r, lam_r, g_out_pad_r = row(lru_conv_b), row(lru_lambda), row(g_out_pad)
    g_final_r = g_final[None, :]

    ck_t = jnp.transpose(cache_k, (0, 1, 3, 4, 2)).reshape(DEPTH, n_seq, D_KV, WINDOW)
    cv_t = jnp.transpose(cache_v, (0, 1, 3, 4, 2)).reshape(DEPTH, n_seq, D_KV, WINDOW)
    lbuf_t = jnp.transpose(state_lru_conv, (0, 2, 1, 3))
    ys, ck_new, cv_new, h_new, lbuf_new, sbuf_new, fst_new = _sample_step(
        x_sample.reshape(n_seq, D_MODEL), ck_t, cv_t, state_rglru, lbuf_t, state_sconv,
        state_ffn_conv, sink_rows, g_mix_r, w_q_pad, w_in_b, w_kvt, lru_conv_w, lcb_r, w_gates,
        b_gates, lam_r, sc_conv_w, g_out_pad_r, w_out_pad, g_ffn_r, w_up_b, ffn_conv_w, w_down_b,
        g_final_r)
    unview = lambda c: jnp.transpose(
        c.reshape(DEPTH, n_seq, N_KV_HEADS, HEAD_DIM, WINDOW), (0, 1, 4, 2, 3))
    s_out = (unview(ck_new), unview(cv_new), h_new, jnp.transpose(lbuf_new, (0, 2, 1, 3)),
             sbuf_new, fst_new)

    xp = x_prompt
    p_states = []
    for l in range(DEPTH):
        xp, kv_new, h_last, lbuf_p, sbuf_p, fbuf_p = _prompt_layer(
            l, xp, sinks, g_mix_r, w_in_b, lru_conv_w, lcb_r, w_gates, b_gates, lam_r, sc_conv_w,
            g_out_r, w_out_b, g_ffn_r, w_up_b, ffn_conv_w, w_down_b, g_final_r,
            final=(l == DEPTH - 1))
        p_states.append((
            kv_new[:, :, :D_KV].reshape(batch, WINDOW, N_KV_HEADS, HEAD_DIM),
            kv_new[:, :, D_KV:].reshape(batch, WINDOW, N_KV_HEADS, HEAD_DIM),
            h_last.reshape(batch, D_LRU), lbuf_p, sbuf_p, fbuf_p))

    stack = lambda i: jnp.stack([st[i] for st in p_states])
    return ((xp, ys.reshape(n_seq, 1, D_MODEL)) + tuple(stack(i) for i in range(6)) + s_out)
```

```python
import functools

import jax
import jax.numpy as jnp
from jax import lax
from jax.experimental import pallas as pl
from jax.experimental.pallas import tpu as pltpu

D_MODEL = 1024
DEPTH = 4
HEAD_DIM = 64
N_HEADS = 8
N_KV_HEADS = 2
GQA_GROUP = N_HEADS // N_KV_HEADS
WINDOW = 128
D_ATTN = N_HEADS * HEAD_DIM
D_KV = N_KV_HEADS * HEAD_DIM
D_LRU = 256
N_LRU_BLOCKS = 4
LRU_BLOCK = D_LRU // N_LRU_BLOCKS
LRU_CONV = 4
LRU_C = 8.0
D_SC = 256
SC_CONV = 3
D_MIX = D_ATTN + D_LRU + D_SC
D_IN = D_ATTN + 2 * D_KV + 2 * D_LRU + 3 * D_SC
D_FF = 2816
FFN_CONV = 3
EPS = 1e-6

_Q0, _K0, _V0 = 0, D_ATTN, D_ATTN + D_KV
_LX0 = D_ATTN + 2 * D_KV
_LG0 = _LX0 + D_LRU
_SB0 = _LG0 + D_LRU
_SC0 = _SB0 + D_SC
_SU0 = _SC0 + D_SC

SUBLANES = 8
LANES = 128
TOKEN_BLOCK = 512
FF_CHUNK = 256
N_FF_CHUNKS = D_FF // FF_CHUNK
SEQ_GROUP = 8
D_QPAD = N_HEADS * D_KV
VMEM_LIMIT_BYTES = 56 * 1024 * 1024

_F32 = jnp.float32
_BF16 = jnp.bfloat16
_NT = (((1,), (1,)), ((), ()))

def _rms(x, g):
    ms = jnp.mean(x * x, axis=-1, keepdims=True)
    return x * lax.rsqrt(ms + EPS) * g


def _softplus(x):
    return jnp.maximum(x, 0.0) + jnp.log1p(jnp.exp(-jnp.abs(x)))


def _lru_coeffs(xc, gates, lam):
    r = jax.nn.sigmoid(gates[:, :D_LRU])
    i = jax.nn.sigmoid(gates[:, D_LRU:])
    log_a = (-LRU_C) * r * _softplus(-lam)
    a = jnp.exp(log_a)
    b = jnp.sqrt(jnp.tanh(-log_a) * (1.0 + a * a)) * (i * xc)
    return a, b


def _scan_rows(a, b, scr):
    n, c = a.shape
    a_t = [a[:, j * LANES:(j + 1) * LANES] for j in range(c // LANES)]
    b_t = [b[:, j * LANES:(j + 1) * LANES] for j in range(c // LANES)]
    d = 1
    while d < SUBLANES:
        for j in range(c // LANES):
            scr[0, j, SUBLANES:SUBLANES + n, :] = a_t[j]
            scr[1, j, SUBLANES:SUBLANES + n, :] = b_t[j]
            a_prev = scr[0, j, SUBLANES - d:SUBLANES - d + n, :]
            b_prev = scr[1, j, SUBLANES - d:SUBLANES - d + n, :]
            b_t[j] = a_t[j] * b_prev + b_t[j]
            a_t[j] = a_t[j] * a_prev
        d *= 2
    a = jnp.concatenate(a_t, axis=-1)
    b = jnp.concatenate(b_t, axis=-1)
    while d < n:
        b = jnp.concatenate([b[:d], a[d:] * b[:-d] + b[d:]], axis=0)
        a = jnp.concatenate([a[:d], a[d:] * a[:-d]], axis=0)
        d *= 2
    return a, b


def _causal_conv(u, w_ref, col0, scr, hist=None):
    n, c = u.shape
    taps = w_ref.shape[0]
    h0 = SUBLANES - (taps - 1)
    outs, tails = [], []
    for j in range(c // LANES):
        wcols = slice(col0 + j * LANES, col0 + (j + 1) * LANES)
        uj = u[:, j * LANES:(j + 1) * LANES]
        scr[j, SUBLANES:SUBLANES + n, :] = uj
        if hist is not None:
            scr[j, h0:SUBLANES, :] = hist[j, h0:SUBLANES, :]
        y = w_ref[taps - 1:taps, wcols] * uj
        for k in range(taps - 1):
            y = y + w_ref[k:k + 1, wcols] * scr[j, h0 + k:h0 + k + n, :]
        tail = scr[j, n + h0:n + SUBLANES, :]
        (scr if hist is None else hist)[j, h0:SUBLANES, :] = tail
        outs.append(y)
        tails.append(tail)
    return jnp.concatenate(outs, axis=-1), jnp.concatenate(tails, axis=-1)


def _prompt_layer_kernel(
        sinks_ref, x_ref, g_mix_ref, w_in_ref, lcw_ref, lcb_ref, w_gates_ref, b_gates_ref,
        lam_ref, scw_ref, g_out_ref, w_out_ref, g_ffn_ref, w_up_ref, fcw_ref, w_down_ref,
        g_final_ref,
        y_ref, y_last_ref, kv_out_ref, h_out_ref, lbuf_out_ref, sbuf_out_ref, fbuf_out_ref,
        kv_scr, lc_scr, sc_scr, h_scr, scan_scr, up_scr, fhist_scr, x1_scr, hn_scr, act_scr,
        *, layer, final, blocks_per_seq, n_blocks):
    tb = TOKEN_BLOCK
    i = pl.program_id(0)
    t = lax.rem(i, blocks_per_seq)

    def finish_cols(cols):
        return x1_scr[:, cols] + jnp.dot(act_scr[...], w_down_ref[:, cols],
                                         preferred_element_type=_F32)

    @pl.when(i == 0)
    def _():
        x1_scr[...] = jnp.zeros_like(x1_scr)
        act_scr[...] = jnp.zeros_like(act_scr)
        ident = (tb // WINDOW, 1, D_LRU // LANES, SUBLANES, LANES)
        scan_scr[:, 0:1, :, 0:SUBLANES, :] = jnp.ones(ident, _F32)
        scan_scr[:, 1:2, :, 0:SUBLANES, :] = jnp.zeros(ident, _F32)

    @pl.when(t == 0)
    def _():
        kv_scr[0:WINDOW, :] = jnp.zeros((WINDOW, 2 * D_KV), _BF16)
        lc_scr[:, 0:SUBLANES, :] = jnp.zeros((D_LRU // LANES, SUBLANES, LANES), _F32)
        sc_scr[:, 0:SUBLANES, :] = jnp.zeros((D_SC // LANES, SUBLANES, LANES), _F32)
        h_scr[...] = jnp.zeros_like(h_scr)
        fhist_scr[...] = jnp.zeros_like(fhist_scr)

    @pl.when(t > 0)
    def _():
        kv_scr[0:WINDOW, :] = kv_scr[tb:tb + WINDOW, :]

    x = x_ref[0]
    hb = _rms(x, g_mix_ref[...]).astype(_BF16)

    zf = jnp.dot(hb, w_in_ref[...], preferred_element_type=_F32)
    z = {c0: zf[:, c0:c0 + D_LRU] for c0 in (_LX0, _LG0, _SB0, _SC0, _SU0)}

    q = (zf[:, _Q0:_Q0 + D_ATTN] * (HEAD_DIM ** -0.5)).astype(_BF16)
    kv_scr[WINDOW:WINDOW + tb, :] = zf[:, _K0:_K0 + 2 * D_KV].astype(_BF16)
    kv_out_ref[0] = zf[tb - WINDOW:tb, _K0:_K0 + 2 * D_KV]

    qi = lax.broadcasted_iota(jnp.int32, (WINDOW, 2 * WINDOW), 0)
    kj = lax.broadcasted_iota(jnp.int32, (WINDOW, 2 * WINDOW), 1)
    band = (kj >= qi) & (kj <= qi + WINDOW)
    band_first = band & (kj >= WINDOW * (1 - jnp.minimum(t, 1)))

    a_blocks = []
    n_sub = tb // WINDOW
    for j in range(n_sub):
        keys = kv_scr[j * WINDOW:(j + 2) * WINDOW, :]
        qj = q[j * WINDOW:(j + 1) * WINDOW, :]
        mask = band_first if j == 0 else band
        outs = []
        for h in range(N_HEADS):
            g = h // GQA_GROUP
            sink = sinks_ref[layer, h]
            s = lax.dot_general(qj[:, h * HEAD_DIM:(h + 1) * HEAD_DIM],
                                keys[:, g * HEAD_DIM:(g + 1) * HEAD_DIM], _NT,
                                preferred_element_type=_F32)
            s = jnp.where(mask, s, -jnp.inf)
            m = jnp.maximum(jnp.max(s, axis=-1, keepdims=True), sink)
            p = jnp.exp(s - m)
            denom = jnp.sum(p, axis=-1, keepdims=True) + jnp.exp(sink - m)
            o = jnp.dot(p.astype(_BF16),
                        keys[:, D_KV + g * HEAD_DIM:D_KV + (g + 1) * HEAD_DIM],
                        preferred_element_type=_F32)
            outs.append(o / denom)
        a_blocks.append(jnp.concatenate(outs, axis=-1))
    a_out = jnp.concatenate(a_blocks, axis=0)

    xc, tail = _causal_conv(z[_LX0], lcw_ref, 0, lc_scr)
    xc = xc + lcb_ref[...]
    lbuf_out_ref[0] = tail

    gates = jnp.dot(xc.astype(_BF16), w_gates_ref[...],
                    preferred_element_type=_F32) + b_gates_ref[...]
    a_dec, b_in = _lru_coeffs(xc, gates, lam_ref[...])
    carry = h_scr[0:1, :]
    h_blocks = []
    for j in range(n_sub):
        rows = slice(j * WINDOW, (j + 1) * WINDOW)
        a_cum, b_cum = _scan_rows(a_dec[rows], b_in[rows], scan_scr.at[j])
        hj = a_cum * carry + b_cum
        carry = hj[WINDOW - 1:WINDOW, :]
        h_blocks.append(hj)
        cols = slice(j * (D_MODEL // n_sub), (j + 1) * (D_MODEL // n_sub))
        y_ref[0, :, cols] = finish_cols(cols)
    if final:
        y_ref[0] = _rms(y_ref[0], g_final_ref[...])
    hseq = jnp.concatenate(h_blocks, axis=0)
    h_scr[0:1, :] = carry
    h_out_ref[0] = carry
    b_out = hseq * jax.nn.gelu(z[_LG0])

    cu = z[_SC0] * z[_SU0]
    cconv, tail = _causal_conv(cu, scw_ref, 0, sc_scr)
    sbuf_out_ref[0] = tail
    c_out = z[_SB0] * cconv

    g_out = g_out_ref[...]
    merged = jnp.concatenate([
        _rms(a_out, g_out[:, :D_ATTN]),
        _rms(b_out, g_out[:, D_ATTN:D_ATTN + D_LRU]),
        _rms(c_out, g_out[:, D_ATTN + D_LRU:])], axis=-1).astype(_BF16)
    x1 = x + jnp.dot(merged, w_out_ref[...], preferred_element_type=_F32)

    hn_scr[...] = _rms(x1, g_ffn_ref[...]).astype(_BF16)
    x1_scr[...] = x1
    n_t = FF_CHUNK // LANES
    for c in range(N_FF_CHUNKS):
        cu0, cg0 = c * FF_CHUNK, D_FF + c * FF_CHUNK
        slot = c % 2
        up_u = jnp.dot(hn_scr[...], w_up_ref[:, cu0:cu0 + FF_CHUNK], preferred_element_type=_F32)
        up_g = jnp.dot(hn_scr[...], w_up_ref[:, cg0:cg0 + FF_CHUNK], preferred_element_type=_F32)
        uc, tail_u = _causal_conv(up_u, fcw_ref, cu0, up_scr.at[slot, 0:n_t],
                                  hist=fhist_scr.at[c, 0:n_t])
        gc, tail_g = _causal_conv(up_g, fcw_ref, cg0, up_scr.at[slot, n_t:2 * n_t],
                                  hist=fhist_scr.at[c, n_t:2 * n_t])
        fbuf_out_ref[0, :, cu0:cu0 + FF_CHUNK] = tail_u
        fbuf_out_ref[0, :, cg0:cg0 + FF_CHUNK] = tail_g
        act_scr[:, cu0:cu0 + FF_CHUNK] = (jax.nn.silu(gc) * uc).astype(_BF16)

    @pl.when(i == n_blocks - 1)
    def _():
        x2 = finish_cols(slice(0, D_MODEL))
        y_last_ref[...] = _rms(x2, g_final_ref[...]) if final else x2


def _layer_spec(a, layer):
    zeros = (0,) * (a.ndim - 1)
    return pl.BlockSpec((None,) + a.shape[1:], lambda *_: (layer,) + zeros,
                        pipeline_mode=pl.Buffered(1))


def _prompt_layer(layer, x, sinks, g_mix, w_in, lcw, lcb, w_gates, b_gates, lam, scw, g_out, w_out,
                  g_ffn, w_up, fcw, w_down, g_final, *, final):
    batch, seq, _ = x.shape
    tb = TOKEN_BLOCK
    stacked = (g_mix, w_in, lcw, lcb, w_gates, b_gates, lam, scw, g_out, w_out, g_ffn, w_up,
               fcw, w_down)
    bps = seq // tb
    n_blocks = batch * bps

    def cur(i):
        return (i // bps, lax.rem(i, bps), 0)

    def prev(i):
        return cur(jnp.maximum(i - 1, 0))

    def seq_of(i):
        return (i // bps, 0, 0)

    in_specs = [pl.BlockSpec(memory_space=pltpu.SMEM),
                pl.BlockSpec((1, tb, D_MODEL), cur)]
    in_specs += [_layer_spec(a, layer) for a in stacked]
    in_specs += [pl.BlockSpec(g_final.shape, lambda i: (0, 0), pipeline_mode=pl.Buffered(1))]
    out_shape = (
        jax.ShapeDtypeStruct((batch, seq, D_MODEL), _F32),
        jax.ShapeDtypeStruct((tb, D_MODEL), _F32),
        jax.ShapeDtypeStruct((batch, WINDOW, 2 * D_KV), _F32),
        jax.ShapeDtypeStruct((batch, 1, D_LRU), _F32),
        jax.ShapeDtypeStruct((batch, LRU_CONV - 1, D_LRU), _F32),
        jax.ShapeDtypeStruct((batch, SC_CONV - 1, D_SC), _F32),
        jax.ShapeDtypeStruct((batch, FFN_CONV - 1, 2 * D_FF), _F32),
    )
    out_specs = (
        pl.BlockSpec((1, tb, D_MODEL), prev),
        pl.BlockSpec((tb, D_MODEL), lambda i: (0, 0)),
        pl.BlockSpec((1, WINDOW, 2 * D_KV), seq_of),
        pl.BlockSpec((1, 1, D_LRU), seq_of),
        pl.BlockSpec((1, LRU_CONV - 1, D_LRU), seq_of),
        pl.BlockSpec((1, SC_CONV - 1, D_SC), seq_of),
        pl.BlockSpec((1, FFN_CONV - 1, 2 * D_FF), seq_of),
    )
    n_t = 2 * FF_CHUNK // LANES
    scratch = [
        pltpu.VMEM((WINDOW + tb, 2 * D_KV), _BF16),
        pltpu.VMEM((D_LRU // LANES, SUBLANES + tb, LANES), _F32),
        pltpu.VMEM((D_SC // LANES, SUBLANES + tb, LANES), _F32),
        pltpu.VMEM((SUBLANES, D_LRU), _F32),
        pltpu.VMEM((tb // WINDOW, 2, D_LRU // LANES, SUBLANES + WINDOW, LANES), _F32),
        pltpu.VMEM((2, n_t, SUBLANES + tb, LANES), _F32),
        pltpu.VMEM((N_FF_CHUNKS, n_t, SUBLANES, LANES), _F32),
        pltpu.VMEM((tb, D_MODEL), _F32),
        pltpu.VMEM((tb, D_MODEL), _BF16),
        pltpu.VMEM((tb, D_FF), _BF16),
    ]
    y, y_last, *states = pl.pallas_call(
        functools.partial(_prompt_layer_kernel, layer=layer, final=final, blocks_per_seq=bps,
                          n_blocks=n_blocks),
        grid=(n_blocks,),
        in_specs=in_specs,
        out_specs=out_specs,
        out_shape=out_shape,
        scratch_shapes=scratch,
        compiler_params=pltpu.CompilerParams(
            dimension_semantics=("arbitrary",),
            vmem_limit_bytes=VMEM_LIMIT_BYTES),
        name="prompt_layer",
    )(sinks, x, *stacked, g_final)
    y = lax.dynamic_update_slice(y, y_last[None], (batch - 1, seq - tb, 0))
    return (y, *states)


def _sample_kernel(
        x_ref, ck_ref, cv_ref, h0_ref, lbuf_ref, sbuf_ref, fst_ref, sink_rows_ref,
        g_mix_ref, w_q_ref, w_in_ref, lcw_ref, lcb_ref, w_gates_ref, b_gates_ref,
        lam_ref, scw_ref, g_out_ref, w_out_ref, g_ffn_ref, w_up_ref, fcw_ref, w_down_ref,
        g_final_ref,
        y_ref, ck_out_ref, cv_out_ref, h_out_ref, lbuf_out_ref, sbuf_out_ref, fst_out_ref,
        xs_scr, hb_scr, q_scr, rest_scr, o_scr, uc_scr, acc_scr,
        *, n_groups):
    l = pl.program_id(0)
    s = pl.program_id(1)
    gs = SEQ_GROUP
    n_blocks = 2 * N_FF_CHUNKS
    last = n_groups + n_blocks - 1

    @pl.when((l == 0) & (s == 0))
    def _():
        xs_scr[...] = x_ref[...]

    @pl.when(s == 0)
    def _():
        hb = _rms(xs_scr[...], g_mix_ref[...]).astype(_BF16)
        hb_scr[...] = hb
        q_scr[...] = jnp.dot(hb, w_q_ref[...], preferred_element_type=_F32) * (HEAD_DIM ** -0.5)
        rest_scr[...] = jnp.dot(hb, w_in_ref[:, D_ATTN:], preferred_element_type=_F32)

    @pl.when(s < n_groups)
    def _():
        row0 = pl.multiple_of(s * gs, gs)
        q_all = jnp.concatenate(
            [q_scr[pl.ds(row0, gs), h * D_KV:(h + 1) * D_KV] for h in range(N_HEADS)], axis=0)
        k_new = rest_scr[pl.ds(row0, gs), 0:D_KV]
        v_new = rest_scr[pl.ds(row0, gs), D_KV:2 * D_KV]
        k_new_rows = jnp.concatenate([k_new] * N_HEADS, axis=0)
        v_new_rows = jnp.concatenate([v_new] * N_HEADS, axis=0)

        keys_t = jnp.concatenate([ck_ref[b] for b in range(gs)], axis=-1).astype(_BF16)
        s_all = jnp.dot(q_all.astype(_BF16), keys_t, preferred_element_type=_F32)
        seq_of_row = lax.broadcasted_iota(jnp.int32, (N_HEADS * gs, WINDOW), 0) & (gs - 1)
        sc = jnp.zeros((N_HEADS * gs, WINDOW), _F32)
        for j in range(gs):
            sc = sc + jnp.where(seq_of_row == j, s_all[:, j * WINDOW:(j + 1) * WINDOW], 0.0)
        s_new = jnp.sum(q_all * k_new_rows, axis=-1, keepdims=True)
        sink = sink_rows_ref[:, 0:1]
        m = jnp.maximum(jnp.maximum(jnp.max(sc, axis=-1, keepdims=True), s_new), sink)
        p = jnp.exp(sc - m)
        p_new = jnp.exp(s_new - m)
        denom = jnp.sum(p, axis=-1, keepdims=True) + p_new + jnp.exp(sink - m)
        p_blk = jnp.concatenate(
            [jnp.where(seq_of_row == j, p, 0.0) for j in range(gs)], axis=-1).astype(_BF16)
        vals_t = jnp.concatenate([cv_ref[b] for b in range(gs)], axis=-1).astype(_BF16)
        o = lax.dot_general(p_blk, vals_t, _NT, preferred_element_type=_F32) + p_new * v_new_rows
        o = o / denom
        row = lax.broadcasted_iota(jnp.int32, (N_HEADS * gs, D_KV), 0)
        lane = lax.broadcasted_iota(jnp.int32, (N_HEADS * gs, D_KV), 1)
        own_kv = (lax.shift_right_logical(row, (GQA_GROUP * gs).bit_length() - 1)
                  == lax.shift_right_logical(lane, HEAD_DIM.bit_length() - 1))
        o = jnp.where(own_kv, o, 0.0)
        for h in range(N_HEADS):
            o_scr[pl.ds(row0, gs), h * D_KV:(h + 1) * D_KV] = o[h * gs:(h + 1) * gs, :]

        kv_new_t = rest_scr[pl.ds(row0, gs), 0:2 * D_KV].T
        pos = lax.broadcasted_iota(jnp.int32, (D_KV, WINDOW), 1)
        for b in range(gs):
            ck_out_ref[b] = jnp.where(pos == WINDOW - 1, kv_new_t[0:D_KV, b:b + 1],
                                      pltpu.roll(ck_ref[b], WINDOW - 1, 1))
            cv_out_ref[b] = jnp.where(pos == WINDOW - 1, kv_new_t[D_KV:2 * D_KV, b:b + 1],
                                      pltpu.roll(cv_ref[b], WINDOW - 1, 1))

    @pl.when(s == n_groups - 1)
    def _():
        rest = rest_scr[...]
        base = 2 * D_KV
        lx = rest[:, base:base + D_LRU]
        lg = rest[:, base + D_LRU:base + 2 * D_LRU]
        sb = rest[:, base + 2 * D_LRU:base + 2 * D_LRU + D_SC]
        scc = rest[:, base + 2 * D_LRU + D_SC:base + 2 * D_LRU + 2 * D_SC]
        su = rest[:, base + 2 * D_LRU + 2 * D_SC:]
        xc = lcb_ref[...] + lcw_ref[LRU_CONV - 1:LRU_CONV, :] * lx
        for k in range(LRU_CONV - 1):
            xc = xc + lcw_ref[k:k + 1, :] * lbuf_ref[k]
        for k in range(LRU_CONV - 2):
            lbuf_out_ref[k] = lbuf_ref[k + 1]
        lbuf_out_ref[LRU_CONV - 2] = lx
        gates = jnp.dot(xc.astype(_BF16), w_gates_ref[...],
                        preferred_element_type=_F32) + b_gates_ref[...]
        a_dec, b_in = _lru_coeffs(xc, gates, lam_ref[...])
        h_new = a_dec * h0_ref[...] + b_in
        h_out_ref[...] = h_new
        b_out = h_new * jax.nn.gelu(lg)
        cu = scc * su
        cconv = scw_ref[SC_CONV - 1:SC_CONV, :] * cu
        for k in range(SC_CONV - 1):
            cconv = cconv + scw_ref[k:k + 1, :] * sbuf_ref[:, k, :]
        for k in range(SC_CONV - 2):
            sbuf_out_ref[:, k, :] = sbuf_ref[:, k + 1, :]
        sbuf_out_ref[:, SC_CONV - 2, :] = cu
        c_out = sb * cconv
        g_out = g_out_ref[...]
        o_pad = o_scr[...]
        a_scale = lax.rsqrt(jnp.sum(o_pad * o_pad, axis=-1, keepdims=True) / D_ATTN + EPS)
        merged = jnp.concatenate([
            o_pad * a_scale * g_out[:, :D_QPAD],
            _rms(b_out, g_out[:, D_QPAD:D_QPAD + D_LRU]),
            _rms(c_out, g_out[:, D_QPAD + D_LRU:])], axis=-1).astype(_BF16)
        x1 = xs_scr[...] + jnp.dot(merged, w_out_ref[...], preferred_element_type=_F32)
        hb_scr[...] = _rms(x1, g_ffn_ref[...]).astype(_BF16)
        acc_scr[...] = x1

    @pl.when(s >= n_groups)
    def _():
        j = s - n_groups
        up = jnp.dot(hb_scr[...], w_up_ref[...], preferred_element_type=_F32)
        upc = fcw_ref[FFN_CONV - 1:FFN_CONV, :] * up
        for k in range(FFN_CONV - 1):
            upc = upc + fcw_ref[k:k + 1, :] * fst_ref[:, k, :]
        for k in range(FFN_CONV - 2):
            fst_out_ref[:, k, :] = fst_ref[:, k + 1, :]
        fst_out_ref[:, FFN_CONV - 2, :] = up

        @pl.when(j < N_FF_CHUNKS)
        def _():
            uc_scr[j] = upc

        @pl.when(j >= N_FF_CHUNKS)
        def _():
            act = (jax.nn.silu(upc) * uc_scr[j - N_FF_CHUNKS]).astype(_BF16)
            acc_scr[...] += jnp.dot(act, w_down_ref[...], preferred_element_type=_F32)

    @pl.when(s == last)
    def _():
        xs_scr[...] = acc_scr[...]

    @pl.when((s == last) & (l == DEPTH - 1))
    def _():
        y_ref[...] = _rms(acc_scr[...], g_final_ref[...])


def _sample_step(x, ck_t, cv_t, h0, lbuf, sbuf, fstate, sink_rows, g_mix, w_q, w_in, lcw,
                 lcb, w_gates, b_gates, lam, scw, g_out_pad, w_out_pad, g_ffn, w_up, fcw, w_down,
                 g_final):
    n_seq = x.shape[0]
    gs = SEQ_GROUP
    n_groups = n_seq // gs
    n_blocks = 2 * N_FF_CHUNKS
    steps = n_groups + n_blocks

    def per_layer(a):
        zeros = (0,) * (a.ndim - 1)
        return pl.BlockSpec((None,) + a.shape[1:], lambda l, s: (l,) + zeros)

    def group(l, s):
        return (l, jnp.minimum(s, n_groups - 1), 0, 0)

    def ff_block(l, s):
        return jnp.clip(s - n_groups, 0, n_blocks - 1)

    cache_spec = pl.BlockSpec((None, gs, D_KV, WINDOW), group)
    fst_spec = pl.BlockSpec((None, n_seq, FFN_CONV - 1, FF_CHUNK), lambda l, s: (l, 0, 0, ff_block(l, s)))
    in_specs = [
        pl.BlockSpec(x.shape, lambda l, s: (0, 0)),
        cache_spec, cache_spec, per_layer(h0), per_layer(lbuf), per_layer(sbuf), fst_spec,
        per_layer(sink_rows), per_layer(g_mix), per_layer(w_q), per_layer(w_in),
        per_layer(lcw), per_layer(lcb), per_layer(w_gates), per_layer(b_gates), per_layer(lam),
        per_layer(scw), per_layer(g_out_pad), per_layer(w_out_pad), per_layer(g_ffn),
        pl.BlockSpec((None, D_MODEL, FF_CHUNK), lambda l, s: (l, 0, ff_block(l, s))),
        pl.BlockSpec((None, FFN_CONV, FF_CHUNK), lambda l, s: (l, 0, ff_block(l, s))),
        pl.BlockSpec((None, FF_CHUNK, D_MODEL),
                     lambda l, s: (l, jnp.clip(s - n_groups - N_FF_CHUNKS, 0, N_FF_CHUNKS - 1), 0)),
        pl.BlockSpec(g_final.shape, lambda l, s: (0, 0)),
    ]
    out_shape = (
        jax.ShapeDtypeStruct((n_seq, D_MODEL), _F32),
        jax.ShapeDtypeStruct(ck_t.shape, _F32),
        jax.ShapeDtypeStruct(cv_t.shape, _F32),
        jax.ShapeDtypeStruct(h0.shape, _F32),
        jax.ShapeDtypeStruct(lbuf.shape, _F32),
        jax.ShapeDtypeStruct(sbuf.shape, _F32),
        jax.ShapeDtypeStruct(fstate.shape, _F32),
    )
    out_specs = (
        pl.BlockSpec((n_seq, D_MODEL), lambda l, s: (0, 0)),
        cache_spec, cache_spec, per_layer(h0), per_layer(lbuf), per_layer(sbuf), fst_spec,
    )
    scratch = [
        pltpu.VMEM((n_seq, D_MODEL), _F32),
        pltpu.VMEM((n_seq, D_MODEL), _BF16),
        pltpu.VMEM((n_seq, D_QPAD), _F32),
        pltpu.VMEM((n_seq, D_IN - D_ATTN), _F32),
        pltpu.VMEM((n_seq, D_QPAD), _F32),
        pltpu.VMEM((N_FF_CHUNKS, n_seq, FF_CHUNK), _F32),
        pltpu.VMEM((n_seq, D_MODEL), _F32),
    ]
    return pl.pallas_call(
        functools.partial(_sample_kernel, n_groups=n_groups),
        grid=(DEPTH, steps),
        in_specs=in_specs,
        out_specs=out_specs,
        out_shape=out_shape,
        scratch_shapes=scratch,
        compiler_params=pltpu.CompilerParams(
            dimension_semantics=("arbitrary", "arbitrary"),
            vmem_limit_bytes=VMEM_LIMIT_BYTES),
        name="sample_step",
    )(x, ck_t, cv_t, h0, lbuf, sbuf, fstate, sink_rows, g_mix, w_q, w_in, lcw, lcb, w_gates,
      b_gates, lam, scw, g_out_pad, w_out_pad, g_ffn, w_up, fcw, w_down, g_final)


def _block_diag(w):
    eye = jnp.eye(N_LRU_BLOCKS, dtype=w.dtype).reshape(1, N_LRU_BLOCKS, 1, N_LRU_BLOCKS, 1)
    return (w[:, :, :, None, :] * eye).reshape(DEPTH, D_LRU, D_LRU)


def _pad_heads(w, axis):
    pieces = []
    for h in range(N_HEADS):
        blk = lax.slice_in_dim(w, h * HEAD_DIM, (h + 1) * HEAD_DIM, axis=axis)
        for g in range(N_KV_HEADS):
            pieces.append(blk if g == h // GQA_GROUP else jnp.zeros_like(blk))
    return jnp.concatenate(pieces, axis=axis)


def kernel(x_prompt, x_sample, cache_k, cache_v, state_rglru, state_lru_conv, state_sconv, state_ffn_conv, g_mix, w_in, sinks, lru_conv_w, lru_conv_b, lru_wa, lru_ba, lru_wi, lru_bi, lru_lambda, sc_conv_w, g_out, w_out, g_ffn, w_up, ffn_conv_w, w_down, g_final):
    batch = x_prompt.shape[0]
    n_seq = x_sample.shape[0]

    w_in_b = w_in.astype(_BF16)
    w_out_b = w_out.astype(_BF16)
    w_up_b = w_up.astype(_BF16)
    w_down_b = w_down.astype(_BF16)
    w_gates = jnp.concatenate([_block_diag(lru_wa), _block_diag(lru_wi)], axis=-1).astype(_BF16)
    b_gates = jnp.concatenate([lru_ba, lru_bi], axis=-1)[:, None, :]
    w_q_pad = _pad_heads(w_in[:, :, :D_ATTN], 2).astype(_BF16)
    w_out_pad = jnp.concatenate([_pad_heads(w_out[:, :D_ATTN], 1), w_out[:, D_ATTN:]],
                                axis=1).astype(_BF16)
    g_out_pad = jnp.concatenate([_pad_heads(g_out[:, :D_ATTN], 1), g_out[:, D_ATTN:]], axis=1)
    sink_rows = jnp.broadcast_to(
        jnp.repeat(sinks, SEQ_GROUP, axis=1)[:, :, None], (DEPTH, N_HEADS * SEQ_GROUP, D_KV))

    row = lambda a: a[:, None, :]
    g_mix_r, g_out_r, g_ffn_r = row(g_mix), row(g_out), row(g_ffn)
    lcb_r, lam_r, g_out_pad_r = row(lru_conv_b), row(lru_lambda), row(g_out_pad)
    g_final_r = g_final[None, :]

    ck_t = jnp.transpose(cache_k, (0, 1, 3, 4, 2)).reshape(DEPTH, n_seq, D_KV, WINDOW)
    cv_t = jnp.transpose(cache_v, (0, 1, 3, 4, 2)).reshape(DEPTH, n_seq, D_KV, WINDOW)
    lbuf_t = jnp.transpose(state_lru_conv, (0, 2, 1, 3))
    ys, ck_new, cv_new, h_new, lbuf_new, sbuf_new, fst_new = _sample_step(
        x_sample.reshape(n_seq, D_MODEL), ck_t, cv_t, state_rglru, lbuf_t, state_sconv,
        state_ffn_conv, sink_rows, g_mix_r, w_q_pad, w_in_b, lru_conv_w, lcb_r, w_gates,
        b_gates, lam_r, sc_conv_w, g_out_pad_r, w_out_pad, g_ffn_r, w_up_b, ffn_conv_w, w_down_b,
        g_final_r)
    unview = lambda c: jnp.transpose(
        c.reshape(DEPTH, n_seq, N_KV_HEADS, HEAD_DIM, WINDOW), (0, 1, 4, 2, 3))
    s_out = (unview(ck_new), unview(cv_new), h_new, jnp.transpose(lbuf_new, (0, 2, 1, 3)),
             sbuf_new, fst_new)

    xp = x_prompt
    p_states = []
    for l in range(DEPTH):
        xp, kv_new, h_last, lbuf_p, sbuf_p, fbuf_p = _prompt_layer(
            l, xp, sinks, g_mix_r, w_in_b, lru_conv_w, lcb_r, w_gates, b_gates, lam_r, sc_conv_w,
            g_out_r, w_out_b, g_ffn_r, w_up_b, ffn_conv_w, w_down_b, g_final_r,
            final=(l == DEPTH - 1))
        p_states.append((
            kv_new[:, :, :D_KV].reshape(batch, WINDOW, N_KV_HEADS, HEAD_DIM),
            kv_new[:, :, D_KV:].reshape(batch, WINDOW, N_KV_HEADS, HEAD_DIM),
            h_last.reshape(batch, D_LRU), lbuf_p, sbuf_p, fbuf_p))

    stack = lambda i: jnp.stack([st[i] for st in p_states])
    return ((xp, ys.reshape(n_seq, 1, D_MODEL)) + tuple(stack(i) for i in range(6)) + s_out)
```

```python
import functools

import jax
import jax.numpy as jnp
from jax import lax
from jax.experimental import pallas as pl
from jax.experimental.pallas import tpu as pltpu

D_MODEL = 1024
DEPTH = 4
HEAD_DIM = 64
N_HEADS = 8
N_KV_HEADS = 2
GQA_GROUP = N_HEADS // N_KV_HEADS
WINDOW = 128
D_ATTN = N_HEADS * HEAD_DIM
D_KV = N_KV_HEADS * HEAD_DIM
D_LRU = 256
N_LRU_BLOCKS = 4
LRU_BLOCK = D_LRU // N_LRU_BLOCKS
LRU_CONV = 4
LRU_C = 8.0
D_SC = 256
SC_CONV = 3
D_MIX = D_ATTN + D_LRU + D_SC
D_IN = D_ATTN + 2 * D_KV + 2 * D_LRU + 3 * D_SC
D_FF = 2816
FFN_CONV = 3
EPS = 1e-6

_Q0, _K0, _V0 = 0, D_ATTN, D_ATTN + D_KV
_LX0 = D_ATTN + 2 * D_KV
_LG0 = _LX0 + D_LRU
_SB0 = _LG0 + D_LRU
_SC0 = _SB0 + D_SC
_SU0 = _SC0 + D_SC

SUBLANES = 8
LANES = 128
TOKEN_BLOCK = 512
FF_CHUNK = 256
N_FF_CHUNKS = D_FF // FF_CHUNK
SEQ_GROUP = 8
D_QPAD = N_HEADS * D_KV
VMEM_LIMIT_BYTES = 60 * 1024 * 1024

_F32 = jnp.float32
_BF16 = jnp.bfloat16
_NT = (((1,), (1,)), ((), ()))

def _rms(x, g):
    ms = jnp.mean(x * x, axis=-1, keepdims=True)
    return x * lax.rsqrt(ms + EPS) * g


def _softplus(x):
    return jnp.maximum(x, 0.0) + jnp.log1p(jnp.exp(-jnp.abs(x)))


def _lru_coeffs(xc, gates, lam):
    r = jax.nn.sigmoid(gates[:, :D_LRU])
    i = jax.nn.sigmoid(gates[:, D_LRU:])
    log_a = (-LRU_C) * r * _softplus(-lam)
    a = jnp.exp(log_a)
    b = jnp.sqrt(jnp.tanh(-log_a) * (1.0 + a * a)) * (i * xc)
    return a, b


def _scan_rows(a, b, scr):
    n, c = a.shape
    a_t = [a[:, j * LANES:(j + 1) * LANES] for j in range(c // LANES)]
    b_t = [b[:, j * LANES:(j + 1) * LANES] for j in range(c // LANES)]
    d = 1
    while d < SUBLANES:
        for j in range(c // LANES):
            scr[0, j, SUBLANES:SUBLANES + n, :] = a_t[j]
            scr[1, j, SUBLANES:SUBLANES + n, :] = b_t[j]
            a_prev = scr[0, j, SUBLANES - d:SUBLANES - d + n, :]
            b_prev = scr[1, j, SUBLANES - d:SUBLANES - d + n, :]
            b_t[j] = a_t[j] * b_prev + b_t[j]
            a_t[j] = a_t[j] * a_prev
        d *= 2
    a = jnp.concatenate(a_t, axis=-1)
    b = jnp.concatenate(b_t, axis=-1)
    while d < n:
        b = jnp.concatenate([b[:d], a[d:] * b[:-d] + b[d:]], axis=0)
        a = jnp.concatenate([a[:d], a[d:] * a[:-d]], axis=0)
        d *= 2
    return a, b


def _causal_conv(u, w_ref, col0, scr, hist=None):
    n, c = u.shape
    taps = w_ref.shape[0]
    h0 = SUBLANES - (taps - 1)
    outs, tails = [], []
    for j in range(c // LANES):
        wcols = slice(col0 + j * LANES, col0 + (j + 1) * LANES)
        uj = u[:, j * LANES:(j + 1) * LANES]
        scr[j, SUBLANES:SUBLANES + n, :] = uj
        if hist is not None:
            scr[j, h0:SUBLANES, :] = hist[j, h0:SUBLANES, :]
        y = w_ref[taps - 1:taps, wcols] * uj
        for k in range(taps - 1):
            y = y + w_ref[k:k + 1, wcols] * scr[j, h0 + k:h0 + k + n, :]
        tail = scr[j, n + h0:n + SUBLANES, :]
        (scr if hist is None else hist)[j, h0:SUBLANES, :] = tail
        outs.append(y)
        tails.append(tail)
    return jnp.concatenate(outs, axis=-1), jnp.concatenate(tails, axis=-1)


def _prompt_layer_kernel(
        sinks_ref, x_ref, xn_ref, g_mix_ref, w_in_ref, lcw_ref, lcb_ref, w_gates_ref, b_gates_ref,
        lam_ref, scw_ref, g_out_ref, w_out_ref, g_ffn_ref, w_up_ref, fcw_ref, w_down_ref,
        g_final_ref,
        y_ref, kv_out_ref, h_out_ref, lbuf_out_ref, sbuf_out_ref, fbuf_out_ref,
        kv_scr, q_scr, zf_scr, kvn_scr, kvt_scr, lc_scr, sc_scr, h_scr, scan_scr, up_scr, fhist_scr,
        x1_scr, hn_scr, act_scr,
        *, layer, final, blocks_per_seq, n_blocks):
    tb = TOKEN_BLOCK
    i = pl.program_id(0)
    t = lax.rem(i, blocks_per_seq)
    slot = lax.rem(i, 2)
    piece = D_LRU
    assert D_ATTN % piece == 0 and 2 * D_KV == piece and D_SC == piece

    def finish_cols(cols):
        return x1_scr[:, cols] + jnp.dot(act_scr[...], w_down_ref[:, cols],
                                         preferred_element_type=_F32)

    def project_piece(hbx, col0, dst):
        pz = jnp.dot(hbx, w_in_ref[:, col0:col0 + piece], preferred_element_type=_F32)
        if col0 < _K0:
            q_scr[dst, :, col0:col0 + piece] = (pz * (HEAD_DIM ** -0.5)).astype(_BF16)
        elif col0 == _K0:
            kvn_scr[...] = pz.astype(_BF16)
            kvt_scr[...] = pz[tb - WINDOW:tb, :]
        else:
            zf_scr[dst, :, col0 - _LX0:col0 - _LX0 + piece] = pz

    @pl.when(i == 0)
    def _():
        x1_scr[...] = jnp.zeros_like(x1_scr)
        act_scr[...] = jnp.zeros_like(act_scr)
        ident = (tb // WINDOW, 1, D_LRU // LANES, SUBLANES, LANES)
        scan_scr[:, 0:1, :, 0:SUBLANES, :] = jnp.ones(ident, _F32)
        scan_scr[:, 1:2, :, 0:SUBLANES, :] = jnp.zeros(ident, _F32)
        hb0 = _rms(x_ref[0], g_mix_ref[...]).astype(_BF16)
        for col0 in range(0, D_IN, piece):
            project_piece(hb0, col0, 0)

    @pl.when(t == 0)
    def _():
        kv_scr[0:WINDOW, :] = jnp.zeros((WINDOW, 2 * D_KV), _BF16)
        lc_scr[:, 0:SUBLANES, :] = jnp.zeros((D_LRU // LANES, SUBLANES, LANES), _F32)
        sc_scr[:, 0:SUBLANES, :] = jnp.zeros((D_SC // LANES, SUBLANES, LANES), _F32)
        h_scr[...] = jnp.zeros_like(h_scr)
        fhist_scr[...] = jnp.zeros_like(fhist_scr)

    @pl.when(t > 0)
    def _():
        kv_scr[0:WINDOW, :] = kv_scr[tb:tb + WINDOW, :]

    @pl.when(i < n_blocks)
    def _():
        _prompt_block(
            slot, t, sinks_ref, x_ref, xn_ref, g_mix_ref, lcw_ref, lcb_ref, w_gates_ref,
            b_gates_ref, lam_ref, scw_ref, g_out_ref, w_out_ref, g_ffn_ref, w_up_ref, fcw_ref,
            g_final_ref, y_ref, kv_out_ref, h_out_ref, lbuf_out_ref, sbuf_out_ref, fbuf_out_ref,
            kv_scr, q_scr, zf_scr, kvn_scr, kvt_scr, lc_scr, sc_scr, h_scr, scan_scr, up_scr,
            fhist_scr, x1_scr, hn_scr, act_scr,
            layer=layer, final=final, finish_cols=finish_cols, project_piece=project_piece)

    @pl.when(i == n_blocks)
    def _():
        x2 = finish_cols(slice(0, D_MODEL))
        y_ref[0] = _rms(x2, g_final_ref[...]) if final else x2


def _prompt_block(
        slot, t, sinks_ref, x_ref, xn_ref, g_mix_ref, lcw_ref, lcb_ref, w_gates_ref, b_gates_ref,
        lam_ref, scw_ref, g_out_ref, w_out_ref, g_ffn_ref, w_up_ref, fcw_ref, g_final_ref,
        y_ref, kv_out_ref, h_out_ref, lbuf_out_ref, sbuf_out_ref, fbuf_out_ref,
        kv_scr, q_scr, zf_scr, kvn_scr, kvt_scr, lc_scr, sc_scr, h_scr, scan_scr, up_scr, fhist_scr,
        x1_scr, hn_scr, act_scr,
        *, layer, final, finish_cols, project_piece):
    tb = TOKEN_BLOCK
    piece = D_LRU
    x = x_ref[0]
    hbn = _rms(xn_ref[0], g_mix_ref[...]).astype(_BF16)
    next_pieces = list(range(0, D_IN, piece))

    def zcol(col0):
        return zf_scr[slot, :, col0 - _LX0:col0 - _LX0 + piece]

    q = q_scr[slot]
    kv_scr[WINDOW:WINDOW + tb, :] = kvn_scr[...]
    kv_out_ref[0] = kvt_scr[...]

    qi = lax.broadcasted_iota(jnp.int32, (WINDOW, 2 * WINDOW), 0)
    kj = lax.broadcasted_iota(jnp.int32, (WINDOW, 2 * WINDOW), 1)
    band = (kj >= qi) & (kj <= qi + WINDOW)
    band_first = band & (kj >= WINDOW * (1 - jnp.minimum(t, 1)))

    a_blocks = []
    n_sub = tb // WINDOW
    for j in range(n_sub):
        keys = kv_scr[j * WINDOW:(j + 2) * WINDOW, :]
        qj = q[j * WINDOW:(j + 1) * WINDOW, :]
        mask = band_first if j == 0 else band
        outs = []
        for h in range(N_HEADS):
            g = h // GQA_GROUP
            sink = sinks_ref[layer, h]
            s = lax.dot_general(qj[:, h * HEAD_DIM:(h + 1) * HEAD_DIM],
                                keys[:, g * HEAD_DIM:(g + 1) * HEAD_DIM], _NT,
                                preferred_element_type=_F32)
            s = jnp.where(mask, s, -jnp.inf)
            m = jnp.maximum(jnp.max(s, axis=-1, keepdims=True), sink)
            p = jnp.exp(s - m)
            denom = jnp.sum(p, axis=-1, keepdims=True) + jnp.exp(sink - m)
            o = jnp.dot(p.astype(_BF16),
                        keys[:, D_KV + g * HEAD_DIM:D_KV + (g + 1) * HEAD_DIM],
                        preferred_element_type=_F32)
            outs.append(o / denom)
        a_blocks.append(jnp.concatenate(outs, axis=-1))
        for _ in range(-(-len(next_pieces) // (n_sub - j))):
            project_piece(hbn, next_pieces.pop(0), 1 - slot)
    a_out = jnp.concatenate(a_blocks, axis=0)

    xc, tail = _causal_conv(zcol(_LX0), lcw_ref, 0, lc_scr)
    xc = xc + lcb_ref[...]
    lbuf_out_ref[0] = tail

    gates = jnp.dot(xc.astype(_BF16), w_gates_ref[...],
                    preferred_element_type=_F32) + b_gates_ref[...]
    a_dec, b_in = _lru_coeffs(xc, gates, lam_ref[...])
    carry = h_scr[0:1, :]
    h_blocks = []
    for j in range(n_sub):
        rows = slice(j * WINDOW, (j + 1) * WINDOW)
        a_cum, b_cum = _scan_rows(a_dec[rows], b_in[rows], scan_scr.at[j])
        hj = a_cum * carry + b_cum
        carry = hj[WINDOW - 1:WINDOW, :]
        h_blocks.append(hj)
        cols = slice(j * (D_MODEL // n_sub), (j + 1) * (D_MODEL // n_sub))
        y_ref[0, :, cols] = finish_cols(cols)
    if final:
        y_ref[0] = _rms(y_ref[0], g_final_ref[...])
    hseq = jnp.concatenate(h_blocks, axis=0)
    h_scr[0:1, :] = carry
    h_out_ref[0] = carry
    b_out = hseq * jax.nn.gelu(zcol(_LG0))

    cu = zcol(_SC0) * zcol(_SU0)
    cconv, tail = _causal_conv(cu, scw_ref, 0, sc_scr)
    sbuf_out_ref[0] = tail
    c_out = zcol(_SB0) * cconv

    g_out = g_out_ref[...]
    merged = jnp.concatenate([
        _rms(a_out, g_out[:, :D_ATTN]),
        _rms(b_out, g_out[:, D_ATTN:D_ATTN + D_LRU]),
        _rms(c_out, g_out[:, D_ATTN + D_LRU:])], axis=-1).astype(_BF16)
    x1 = x + jnp.dot(merged, w_out_ref[...], preferred_element_type=_F32)

    hn_scr[...] = _rms(x1, g_ffn_ref[...]).astype(_BF16)
    x1_scr[...] = x1
    n_t = FF_CHUNK // LANES
    for c in range(N_FF_CHUNKS):
        cu0, cg0 = c * FF_CHUNK, D_FF + c * FF_CHUNK
        buf = c % 2
        up_u = jnp.dot(hn_scr[...], w_up_ref[:, cu0:cu0 + FF_CHUNK], preferred_element_type=_F32)
        up_g = jnp.dot(hn_scr[...], w_up_ref[:, cg0:cg0 + FF_CHUNK], preferred_element_type=_F32)
        uc, tail_u = _causal_conv(up_u, fcw_ref, cu0, up_scr.at[buf, 0:n_t],
                                  hist=fhist_scr.at[c, 0:n_t])
        gc, tail_g = _causal_conv(up_g, fcw_ref, cg0, up_scr.at[buf, n_t:2 * n_t],
                                  hist=fhist_scr.at[c, n_t:2 * n_t])
        fbuf_out_ref[0, :, cu0:cu0 + FF_CHUNK] = tail_u
        fbuf_out_ref[0, :, cg0:cg0 + FF_CHUNK] = tail_g
        act_scr[:, cu0:cu0 + FF_CHUNK] = (jax.nn.silu(gc) * uc).astype(_BF16)


def _layer_spec(a, layer):
    zeros = (0,) * (a.ndim - 1)
    return pl.BlockSpec((None,) + a.shape[1:], lambda *_: (layer,) + zeros,
                        pipeline_mode=pl.Buffered(1))


def _prompt_layer(layer, x, sinks, g_mix, w_in, lcw, lcb, w_gates, b_gates, lam, scw, g_out, w_out,
                  g_ffn, w_up, fcw, w_down, g_final, *, final):
    batch, seq, _ = x.shape
    tb = TOKEN_BLOCK
    stacked = (g_mix, w_in, lcw, lcb, w_gates, b_gates, lam, scw, g_out, w_out, g_ffn, w_up,
               fcw, w_down)
    bps = seq // tb
    n_blocks = batch * bps

    def block(i):
        i = jnp.clip(i, 0, n_blocks - 1)
        return (i // bps, lax.rem(i, bps), 0)

    def seq_of(i):
        return (jnp.minimum(i, n_blocks - 1) // bps, 0, 0)

    in_specs = [pl.BlockSpec(memory_space=pltpu.SMEM),
                pl.BlockSpec((1, tb, D_MODEL), block),
                pl.BlockSpec((1, tb, D_MODEL), lambda i: block(i + 1))]
    in_specs += [_layer_spec(a, layer) for a in stacked]
    in_specs += [pl.BlockSpec(g_final.shape, lambda i: (0, 0), pipeline_mode=pl.Buffered(1))]
    out_shape = (
        jax.ShapeDtypeStruct((batch, seq, D_MODEL), _F32),
        jax.ShapeDtypeStruct((batch, WINDOW, 2 * D_KV), _F32),
        jax.ShapeDtypeStruct((batch, 1, D_LRU), _F32),
        jax.ShapeDtypeStruct((batch, LRU_CONV - 1, D_LRU), _F32),
        jax.ShapeDtypeStruct((batch, SC_CONV - 1, D_SC), _F32),
        jax.ShapeDtypeStruct((batch, FFN_CONV - 1, 2 * D_FF), _F32),
    )
    out_specs = (
        pl.BlockSpec((1, tb, D_MODEL), lambda i: block(i - 1)),
        pl.BlockSpec((1, WINDOW, 2 * D_KV), seq_of),
        pl.BlockSpec((1, 1, D_LRU), seq_of),
        pl.BlockSpec((1, LRU_CONV - 1, D_LRU), seq_of),
        pl.BlockSpec((1, SC_CONV - 1, D_SC), seq_of),
        pl.BlockSpec((1, FFN_CONV - 1, 2 * D_FF), seq_of),
    )
    n_t = 2 * FF_CHUNK // LANES
    scratch = [
        pltpu.VMEM((WINDOW + tb, 2 * D_KV), _BF16),
        pltpu.VMEM((2, tb, D_ATTN), _BF16),
        pltpu.VMEM((2, tb, D_IN - _LX0), _F32),
        pltpu.VMEM((tb, 2 * D_KV), _BF16),
        pltpu.VMEM((WINDOW, 2 * D_KV), _F32),
        pltpu.VMEM((D_LRU // LANES, SUBLANES + tb, LANES), _F32),
        pltpu.VMEM((D_SC // LANES, SUBLANES + tb, LANES), _F32),
        pltpu.VMEM((SUBLANES, D_LRU), _F32),
        pltpu.VMEM((tb // WINDOW, 2, D_LRU // LANES, SUBLANES + WINDOW, LANES), _F32),
        pltpu.VMEM((2, n_t, SUBLANES + tb, LANES), _F32),
        pltpu.VMEM((N_FF_CHUNKS, n_t, SUBLANES, LANES), _F32),
        pltpu.VMEM((tb, D_MODEL), _F32),
        pltpu.VMEM((tb, D_MODEL), _BF16),
        pltpu.VMEM((tb, D_FF), _BF16),
    ]
    return pl.pallas_call(
        functools.partial(_prompt_layer_kernel, layer=layer, final=final, blocks_per_seq=bps,
                          n_blocks=n_blocks),
        grid=(n_blocks + 1,),
        in_specs=in_specs,
        out_specs=out_specs,
        out_shape=out_shape,
        scratch_shapes=scratch,
        compiler_params=pltpu.CompilerParams(
            dimension_semantics=("arbitrary",),
            vmem_limit_bytes=VMEM_LIMIT_BYTES),
        name="prompt_layer",
    )(sinks, x, x, *stacked, g_final)


def _sample_kernel(
        x_ref, ck_ref, cv_ref, h0_ref, lbuf_ref, sbuf_ref, fst_ref, sink_rows_ref,
        g_mix_ref, w_q_ref, w_in_ref, lcw_ref, lcb_ref, w_gates_ref, b_gates_ref,
        lam_ref, scw_ref, g_out_ref, w_out_ref, g_ffn_ref, w_up_ref, fcw_ref, w_down_ref,
        g_final_ref,
        y_ref, ck_out_ref, cv_out_ref, h_out_ref, lbuf_out_ref, sbuf_out_ref, fst_out_ref,
        xs_scr, hb_scr, q_scr, rest_scr, o_scr, uc_scr, acc_scr,
        *, n_groups):
    l = pl.program_id(0)
    s = pl.program_id(1)
    gs = SEQ_GROUP
    n_blocks = 2 * N_FF_CHUNKS
    last = n_groups + n_blocks - 1

    @pl.when((l == 0) & (s == 0))
    def _():
        xs_scr[...] = x_ref[...]

    @pl.when(s == 0)
    def _():
        hb = _rms(xs_scr[...], g_mix_ref[...]).astype(_BF16)
        hb_scr[...] = hb
        q_scr[...] = jnp.dot(hb, w_q_ref[...], preferred_element_type=_F32) * (HEAD_DIM ** -0.5)
        rest_scr[...] = jnp.dot(hb, w_in_ref[:, D_ATTN:], preferred_element_type=_F32)

    @pl.when(s < n_groups)
    def _():
        row0 = pl.multiple_of(s * gs, gs)
        q_all = jnp.concatenate(
            [q_scr[pl.ds(row0, gs), h * D_KV:(h + 1) * D_KV] for h in range(N_HEADS)], axis=0)
        k_new = rest_scr[pl.ds(row0, gs), 0:D_KV]
        v_new = rest_scr[pl.ds(row0, gs), D_KV:2 * D_KV]
        k_new_rows = jnp.concatenate([k_new] * N_HEADS, axis=0)
        v_new_rows = jnp.concatenate([v_new] * N_HEADS, axis=0)

        keys_t = jnp.concatenate([ck_ref[b] for b in range(gs)], axis=-1).astype(_BF16)
        s_all = jnp.dot(q_all.astype(_BF16), keys_t, preferred_element_type=_F32)
        seq_of_row = lax.broadcasted_iota(jnp.int32, (N_HEADS * gs, WINDOW), 0) & (gs - 1)
        sc = jnp.zeros((N_HEADS * gs, WINDOW), _F32)
        for j in range(gs):
            sc = sc + jnp.where(seq_of_row == j, s_all[:, j * WINDOW:(j + 1) * WINDOW], 0.0)
        s_new = jnp.sum(q_all * k_new_rows, axis=-1, keepdims=True)
        sink = sink_rows_ref[:, 0:1]
        m = jnp.maximum(jnp.maximum(jnp.max(sc, axis=-1, keepdims=True), s_new), sink)
        p = jnp.exp(sc - m)
        p_new = jnp.exp(s_new - m)
        denom = jnp.sum(p, axis=-1, keepdims=True) + p_new + jnp.exp(sink - m)
        p_blk = jnp.concatenate(
            [jnp.where(seq_of_row == j, p, 0.0) for j in range(gs)], axis=-1).astype(_BF16)
        vals_t = jnp.concatenate([cv_ref[b] for b in range(gs)], axis=-1).astype(_BF16)
        o = lax.dot_general(p_blk, vals_t, _NT, preferred_element_type=_F32) + p_new * v_new_rows
        o = o / denom
        row = lax.broadcasted_iota(jnp.int32, (N_HEADS * gs, D_KV), 0)
        lane = lax.broadcasted_iota(jnp.int32, (N_HEADS * gs, D_KV), 1)
        own_kv = (lax.shift_right_logical(row, (GQA_GROUP * gs).bit_length() - 1)
                  == lax.shift_right_logical(lane, HEAD_DIM.bit_length() - 1))
        o = jnp.where(own_kv, o, 0.0)
        for h in range(N_HEADS):
            o_scr[pl.ds(row0, gs), h * D_KV:(h + 1) * D_KV] = o[h * gs:(h + 1) * gs, :]

        kv_new_t = rest_scr[pl.ds(row0, gs), 0:2 * D_KV].T
        pos = lax.broadcasted_iota(jnp.int32, (D_KV, WINDOW), 1)
        for b in range(gs):
            ck_out_ref[b] = jnp.where(pos == WINDOW - 1, kv_new_t[0:D_KV, b:b + 1],
                                      pltpu.roll(ck_ref[b], WINDOW - 1, 1))
            cv_out_ref[b] = jnp.where(pos == WINDOW - 1, kv_new_t[D_KV:2 * D_KV, b:b + 1],
                                      pltpu.roll(cv_ref[b], WINDOW - 1, 1))

    @pl.when(s == n_groups - 1)
    def _():
        rest = rest_scr[...]
        base = 2 * D_KV
        lx = rest[:, base:base + D_LRU]
        lg = rest[:, base + D_LRU:base + 2 * D_LRU]
        sb = rest[:, base + 2 * D_LRU:base + 2 * D_LRU + D_SC]
        scc = rest[:, base + 2 * D_LRU + D_SC:base + 2 * D_LRU + 2 * D_SC]
        su = rest[:, base + 2 * D_LRU + 2 * D_SC:]
        xc = lcb_ref[...] + lcw_ref[LRU_CONV - 1:LRU_CONV, :] * lx
        for k in range(LRU_CONV - 1):
            xc = xc + lcw_ref[k:k + 1, :] * lbuf_ref[k]
        for k in range(LRU_CONV - 2):
            lbuf_out_ref[k] = lbuf_ref[k + 1]
        lbuf_out_ref[LRU_CONV - 2] = lx
        gates = jnp.dot(xc.astype(_BF16), w_gates_ref[...],
                        preferred_element_type=_F32) + b_gates_ref[...]
        a_dec, b_in = _lru_coeffs(xc, gates, lam_ref[...])
        h_new = a_dec * h0_ref[...] + b_in
        h_out_ref[...] = h_new
        b_out = h_new * jax.nn.gelu(lg)
        cu = scc * su
        cconv = scw_ref[SC_CONV - 1:SC_CONV, :] * cu
        for k in range(SC_CONV - 1):
            cconv = cconv + scw_ref[k:k + 1, :] * sbuf_ref[:, k, :]
        for k in range(SC_CONV - 2):
            sbuf_out_ref[:, k, :] = sbuf_ref[:, k + 1, :]
        sbuf_out_ref[:, SC_CONV - 2, :] = cu
        c_out = sb * cconv
        g_out = g_out_ref[...]
        o_pad = o_scr[...]
        a_scale = lax.rsqrt(jnp.sum(o_pad * o_pad, axis=-1, keepdims=True) / D_ATTN + EPS)
        merged = jnp.concatenate([
            o_pad * a_scale * g_out[:, :D_QPAD],
            _rms(b_out, g_out[:, D_QPAD:D_QPAD + D_LRU]),
            _rms(c_out, g_out[:, D_QPAD + D_LRU:])], axis=-1).astype(_BF16)
        x1 = xs_scr[...] + jnp.dot(merged, w_out_ref[...], preferred_element_type=_F32)
        hb_scr[...] = _rms(x1, g_ffn_ref[...]).astype(_BF16)
        acc_scr[...] = x1

    @pl.when(s >= n_groups)
    def _():
        j = s - n_groups
        up = jnp.dot(hb_scr[...], w_up_ref[...], preferred_element_type=_F32)
        upc = fcw_ref[FFN_CONV - 1:FFN_CONV, :] * up
        for k in range(FFN_CONV - 1):
            upc = upc + fcw_ref[k:k + 1, :] * fst_ref[:, k, :]
        for k in range(FFN_CONV - 2):
            fst_out_ref[:, k, :] = fst_ref[:, k + 1, :]
        fst_out_ref[:, FFN_CONV - 2, :] = up

        @pl.when(j < N_FF_CHUNKS)
        def _():
            uc_scr[j] = upc

        @pl.when(j >= N_FF_CHUNKS)
        def _():
            act = (jax.nn.silu(upc) * uc_scr[j - N_FF_CHUNKS]).astype(_BF16)
            acc_scr[...] += jnp.dot(act, w_down_ref[...], preferred_element_type=_F32)

    @pl.when(s == last)
    def _():
        xs_scr[...] = acc_scr[...]

    @pl.when((s == last) & (l == DEPTH - 1))
    def _():
        y_ref[...] = _rms(acc_scr[...], g_final_ref[...])


def _sample_step(x, ck_t, cv_t, h0, lbuf, sbuf, fstate, sink_rows, g_mix, w_q, w_in, lcw,
                 lcb, w_gates, b_gates, lam, scw, g_out_pad, w_out_pad, g_ffn, w_up, fcw, w_down,
                 g_final):
    n_seq = x.shape[0]
    gs = SEQ_GROUP
    n_groups = n_seq // gs
    n_blocks = 2 * N_FF_CHUNKS
    steps = n_groups + n_blocks

    def per_layer(a):
        zeros = (0,) * (a.ndim - 1)
        return pl.BlockSpec((None,) + a.shape[1:], lambda l, s: (l,) + zeros)

    def group(l, s):
        return (l, jnp.minimum(s, n_groups - 1), 0, 0)

    def ff_block(l, s):
        return jnp.clip(s - n_groups, 0, n_blocks - 1)

    cache_spec = pl.BlockSpec((None, gs, D_KV, WINDOW), group)
    fst_spec = pl.BlockSpec((None, n_seq, FFN_CONV - 1, FF_CHUNK), lambda l, s: (l, 0, 0, ff_block(l, s)))
    in_specs = [
        pl.BlockSpec(x.shape, lambda l, s: (0, 0)),
        cache_spec, cache_spec, per_layer(h0), per_layer(lbuf), per_layer(sbuf), fst_spec,
        per_layer(sink_rows), per_layer(g_mix), per_layer(w_q), per_layer(w_in),
        per_layer(lcw), per_layer(lcb), per_layer(w_gates), per_layer(b_gates), per_layer(lam),
        per_layer(scw), per_layer(g_out_pad), per_layer(w_out_pad), per_layer(g_ffn),
        pl.BlockSpec((None, D_MODEL, FF_CHUNK), lambda l, s: (l, 0, ff_block(l, s))),
        pl.BlockSpec((None, FFN_CONV, FF_CHUNK), lambda l, s: (l, 0, ff_block(l, s))),
        pl.BlockSpec((None, FF_CHUNK, D_MODEL),
                     lambda l, s: (l, jnp.clip(s - n_groups - N_FF_CHUNKS, 0, N_FF_CHUNKS - 1), 0)),
        pl.BlockSpec(g_final.shape, lambda l, s: (0, 0)),
    ]
    out_shape = (
        jax.ShapeDtypeStruct((n_seq, D_MODEL), _F32),
        jax.ShapeDtypeStruct(ck_t.shape, _F32),
        jax.ShapeDtypeStruct(cv_t.shape, _F32),
        jax.ShapeDtypeStruct(h0.shape, _F32),
        jax.ShapeDtypeStruct(lbuf.shape, _F32),
        jax.ShapeDtypeStruct(sbuf.shape, _F32),
        jax.ShapeDtypeStruct(fstate.shape, _F32),
    )
    out_specs = (
        pl.BlockSpec((n_seq, D_MODEL), lambda l, s: (0, 0)),
        cache_spec, cache_spec, per_layer(h0), per_layer(lbuf), per_layer(sbuf), fst_spec,
    )
    scratch = [
        pltpu.VMEM((n_seq, D_MODEL), _F32),
        pltpu.VMEM((n_seq, D_MODEL), _BF16),
        pltpu.VMEM((n_seq, D_QPAD), _F32),
        pltpu.VMEM((n_seq, D_IN - D_ATTN), _F32),
        pltpu.VMEM((n_seq, D_QPAD), _F32),
        pltpu.VMEM((N_FF_CHUNKS, n_seq, FF_CHUNK), _F32),
        pltpu.VMEM((n_seq, D_MODEL), _F32),
    ]
    return pl.pallas_call(
        functools.partial(_sample_kernel, n_groups=n_groups),
        grid=(DEPTH, steps),
        in_specs=in_specs,
        out_specs=out_specs,
        out_shape=out_shape,
        scratch_shapes=scratch,
        compiler_params=pltpu.CompilerParams(
            dimension_semantics=("arbitrary", "arbitrary"),
            vmem_limit_bytes=VMEM_LIMIT_BYTES),
        name="sample_step",
    )(x, ck_t, cv_t, h0, lbuf, sbuf, fstate, sink_rows, g_mix, w_q, w_in, lcw, lcb, w_gates,
      b_gates, lam, scw, g_out_pad, w_out_pad, g_ffn, w_up, fcw, w_down, g_final)


def _block_diag(w):
    eye = jnp.eye(N_LRU_BLOCKS, dtype=w.dtype).reshape(1, N_LRU_BLOCKS, 1, N_LRU_BLOCKS, 1)
    return (w[:, :, :, None, :] * eye).reshape(DEPTH, D_LRU, D_LRU)


def _pad_heads(w, axis):
    pieces = []
    for h in range(N_HEADS):
        blk = lax.slice_in_dim(w, h * HEAD_DIM, (h + 1) * HEAD_DIM, axis=axis)
        for g in range(N_KV_HEADS):
            pieces.append(blk if g == h // GQA_GROUP else jnp.zeros_like(blk))
    return jnp.concatenate(pieces, axis=axis)


def kernel(x_prompt, x_sample, cache_k, cache_v, state_rglru, state_lru_conv, state_sconv, state_ffn_conv, g_mix, w_in, sinks, lru_conv_w, lru_conv_b, lru_wa, lru_ba, lru_wi, lru_bi, lru_lambda, sc_conv_w, g_out, w_out, g_ffn, w_up, ffn_conv_w, w_down, g_final):
    batch = x_prompt.shape[0]
    n_seq = x_sample.shape[0]

    w_in_b = w_in.astype(_BF16)
    w_out_b = w_out.astype(_BF16)
    w_up_b = w_up.astype(_BF16)
    w_down_b = w_down.astype(_BF16)
    w_gates = jnp.concatenate([_block_diag(lru_wa), _block_diag(lru_wi)], axis=-1).astype(_BF16)
    b_gates = jnp.concatenate([lru_ba, lru_bi], axis=-1)[:, None, :]
    w_q_pad = _pad_heads(w_in[:, :, :D_ATTN], 2).astype(_BF16)
    w_out_pad = jnp.concatenate([_pad_heads(w_out[:, :D_ATTN], 1), w_out[:, D_ATTN:]],
                                axis=1).astype(_BF16)
    g_out_pad = jnp.concatenate([_pad_heads(g_out[:, :D_ATTN], 1), g_out[:, D_ATTN:]], axis=1)
    sink_rows = jnp.broadcast_to(
        jnp.repeat(sinks, SEQ_GROUP, axis=1)[:, :, None], (DEPTH, N_HEADS * SEQ_GROUP, D_KV))

    row = lambda a: a[:, None, :]
    g_mix_r, g_out_r, g_ffn_r = row(g_mix), row(g_out), row(g_ffn)
    lcb_r, lam_r, g_out_pad_r = row(lru_conv_b), row(lru_lambda), row(g_out_pad)
    g_final_r = g_final[None, :]

    ck_t = jnp.transpose(cache_k, (0, 1, 3, 4, 2)).reshape(DEPTH, n_seq, D_KV, WINDOW)
    cv_t = jnp.transpose(cache_v, (0, 1, 3, 4, 2)).reshape(DEPTH, n_seq, D_KV, WINDOW)
    lbuf_t = jnp.transpose(state_lru_conv, (0, 2, 1, 3))
    ys, ck_new, cv_new, h_new, lbuf_new, sbuf_new, fst_new = _sample_step(
        x_sample.reshape(n_seq, D_MODEL), ck_t, cv_t, state_rglru, lbuf_t, state_sconv,
        state_ffn_conv, sink_rows, g_mix_r, w_q_pad, w_in_b, lru_conv_w, lcb_r, w_gates,
        b_gates, lam_r, sc_conv_w, g_out_pad_r, w_out_pad, g_ffn_r, w_up_b, ffn_conv_w, w_down_b,
        g_final_r)
    unview = lambda c: jnp.transpose(
        c.reshape(DEPTH, n_seq, N_KV_HEADS, HEAD_DIM, WINDOW), (0, 1, 4, 2, 3))
    s_out = (unview(ck_new), unview(cv_new), h_new, jnp.transpose(lbuf_new, (0, 2, 1, 3)),
             sbuf_new, fst_new)

    xp = x_prompt
    p_states = []
    for l in range(DEPTH):
        xp, kv_new, h_last, lbuf_p, sbuf_p, fbuf_p = _prompt_layer(
            l, xp, sinks, g_mix_r, w_in_b, lru_conv_w, lcb_r, w_gates, b_gates, lam_r, sc_conv_w,
            g_out_r, w_out_b, g_ffn_r, w_up_b, ffn_conv_w, w_down_b, g_final_r,
            final=(l == DEPTH - 1))
        p_states.append((
            kv_new[:, :, :D_KV].reshape(batch, WINDOW, N_KV_HEADS, HEAD_DIM),
            kv_new[:, :, D_KV:].reshape(batch, WINDOW, N_KV_HEADS, HEAD_DIM),
            h_last.reshape(batch, D_LRU), lbuf_p, sbuf_p, fbuf_p))

    stack = lambda i: jnp.stack([st[i] for st in p_states])
    return ((xp, ys.reshape(n_seq, 1, D_MODEL)) + tuple(stack(i) for i in range(6)) + s_out)
```

```python
import functools

import jax
import jax.numpy as jnp
from jax import lax
from jax.experimental import pallas as pl
from jax.experimental.pallas import tpu as pltpu

D_MODEL = 1024
DEPTH = 4
HEAD_DIM = 64
N_HEADS = 8
N_KV_HEADS = 2
GQA_GROUP = N_HEADS // N_KV_HEADS
WINDOW = 128
D_ATTN = N_HEADS * HEAD_DIM
D_KV = N_KV_HEADS * HEAD_DIM
D_LRU = 256
N_LRU_BLOCKS = 4
LRU_BLOCK = D_LRU // N_LRU_BLOCKS
LRU_CONV = 4
LRU_C = 8.0
D_SC = 256
SC_CONV = 3
D_MIX = D_ATTN + D_LRU + D_SC
D_IN = D_ATTN + 2 * D_KV + 2 * D_LRU + 3 * D_SC
D_FF = 2816
FFN_CONV = 3
EPS = 1e-6

_Q0, _K0, _V0 = 0, D_ATTN, D_ATTN + D_KV
_LX0 = D_ATTN + 2 * D_KV
_LG0 = _LX0 + D_LRU
_SB0 = _LG0 + D_LRU
_SC0 = _SB0 + D_SC
_SU0 = _SC0 + D_SC

SUBLANES = 8
LANES = 128
TOKEN_BLOCK = 512
FF_CHUNK = 256
N_FF_CHUNKS = D_FF // FF_CHUNK
SEQ_GROUP = 8
D_QPAD = N_HEADS * D_KV
VMEM_LIMIT_BYTES = 56 * 1024 * 1024

_F32 = jnp.float32
_BF16 = jnp.bfloat16
_NT = (((1,), (1,)), ((), ()))

def _rms(x, g):
    ms = jnp.mean(x * x, axis=-1, keepdims=True)
    return x * lax.rsqrt(ms + EPS) * g


def _softplus(x):
    return jnp.maximum(x, 0.0) + jnp.log1p(jnp.exp(-jnp.abs(x)))


def _lru_coeffs(xc, gates, lam):
    r = jax.nn.sigmoid(gates[:, :D_LRU])
    i = jax.nn.sigmoid(gates[:, D_LRU:])
    log_a = (-LRU_C) * r * _softplus(-lam)
    a = jnp.exp(log_a)
    b = jnp.sqrt(jnp.tanh(-log_a) * (1.0 + a * a)) * (i * xc)
    return a, b


def _scan_rows(a, b, scr):
    n, c = a.shape
    a_t = [a[:, j * LANES:(j + 1) * LANES] for j in range(c // LANES)]
    b_t = [b[:, j * LANES:(j + 1) * LANES] for j in range(c // LANES)]
    d = 1
    while d < SUBLANES:
        for j in range(c // LANES):
            scr[0, j, SUBLANES:SUBLANES + n, :] = a_t[j]
            scr[1, j, SUBLANES:SUBLANES + n, :] = b_t[j]
            a_prev = scr[0, j, SUBLANES - d:SUBLANES - d + n, :]
            b_prev = scr[1, j, SUBLANES - d:SUBLANES - d + n, :]
            b_t[j] = a_t[j] * b_prev + b_t[j]
            a_t[j] = a_t[j] * a_prev
        d *= 2
    a = jnp.concatenate(a_t, axis=-1)
    b = jnp.concatenate(b_t, axis=-1)
    while d < n:
        b = jnp.concatenate([b[:d], a[d:] * b[:-d] + b[d:]], axis=0)
        a = jnp.concatenate([a[:d], a[d:] * a[:-d]], axis=0)
        d *= 2
    return a, b


def _causal_conv(u, w_ref, col0, scr, hist=None):
    n, c = u.shape
    taps = w_ref.shape[0]
    h0 = SUBLANES - (taps - 1)
    outs, tails = [], []
    for j in range(c // LANES):
        wcols = slice(col0 + j * LANES, col0 + (j + 1) * LANES)
        uj = u[:, j * LANES:(j + 1) * LANES]
        scr[j, SUBLANES:SUBLANES + n, :] = uj
        if hist is not None:
            scr[j, h0:SUBLANES, :] = hist[j, h0:SUBLANES, :]
        y = w_ref[taps - 1:taps, wcols] * uj
        for k in range(taps - 1):
            y = y + w_ref[k:k + 1, wcols] * scr[j, h0 + k:h0 + k + n, :]
        tail = scr[j, n + h0:n + SUBLANES, :]
        (scr if hist is None else hist)[j, h0:SUBLANES, :] = tail
        outs.append(y)
        tails.append(tail)
    return jnp.concatenate(outs, axis=-1), jnp.concatenate(tails, axis=-1)


def _prompt_layer_kernel(
        sinks_ref, x_ref, g_mix_ref, w_in_ref, lcw_ref, lcb_ref, w_gates_ref, b_gates_ref,
        lam_ref, scw_ref, g_out_ref, w_out_ref, g_ffn_ref, w_up_ref, fcw_ref, w_down_ref,
        g_final_ref,
        y_ref, kv_out_ref, h_out_ref, lbuf_out_ref, sbuf_out_ref, fbuf_out_ref,
        kv_scr, lc_scr, sc_scr, h_scr, scan_scr, up_scr, fhist_scr, x1_scr, hn_scr, act_scr,
        *, layer, final, blocks_per_seq, n_blocks):
    tb = TOKEN_BLOCK
    i = pl.program_id(0)
    t = lax.rem(i, blocks_per_seq)

    def finish_cols(cols):
        return x1_scr[:, cols] + jnp.dot(act_scr[...], w_down_ref[:, cols],
                                         preferred_element_type=_F32)

    @pl.when(i == 0)
    def _():
        x1_scr[...] = jnp.zeros_like(x1_scr)
        act_scr[...] = jnp.zeros_like(act_scr)
        ident = (tb // WINDOW, 1, D_LRU // LANES, SUBLANES, LANES)
        scan_scr[:, 0:1, :, 0:SUBLANES, :] = jnp.ones(ident, _F32)
        scan_scr[:, 1:2, :, 0:SUBLANES, :] = jnp.zeros(ident, _F32)

    @pl.when(t == 0)
    def _():
        kv_scr[0:WINDOW, :] = jnp.zeros((WINDOW, 2 * D_KV), _BF16)
        lc_scr[:, 0:SUBLANES, :] = jnp.zeros((D_LRU // LANES, SUBLANES, LANES), _F32)
        sc_scr[:, 0:SUBLANES, :] = jnp.zeros((D_SC // LANES, SUBLANES, LANES), _F32)
        h_scr[...] = jnp.zeros_like(h_scr)
        fhist_scr[...] = jnp.zeros_like(fhist_scr)

    @pl.when(t > 0)
    def _():
        kv_scr[0:WINDOW, :] = kv_scr[tb:tb + WINDOW, :]

    @pl.when(i < n_blocks)
    def _():
        _prompt_block(
            t, sinks_ref, x_ref, g_mix_ref, w_in_ref, lcw_ref, lcb_ref, w_gates_ref,
            b_gates_ref, lam_ref, scw_ref, g_out_ref, w_out_ref, g_ffn_ref, w_up_ref, fcw_ref,
            g_final_ref, y_ref, kv_out_ref, h_out_ref, lbuf_out_ref, sbuf_out_ref, fbuf_out_ref,
            kv_scr, lc_scr, sc_scr, h_scr, scan_scr, up_scr, fhist_scr, x1_scr, hn_scr, act_scr,
            layer=layer, final=final, finish_cols=finish_cols)

    @pl.when(i == n_blocks)
    def _():
        x2 = finish_cols(slice(0, D_MODEL))
        y_ref[0] = _rms(x2, g_final_ref[...]) if final else x2


def _prompt_block(
        t, sinks_ref, x_ref, g_mix_ref, w_in_ref, lcw_ref, lcb_ref, w_gates_ref, b_gates_ref,
        lam_ref, scw_ref, g_out_ref, w_out_ref, g_ffn_ref, w_up_ref, fcw_ref, g_final_ref,
        y_ref, kv_out_ref, h_out_ref, lbuf_out_ref, sbuf_out_ref, fbuf_out_ref,
        kv_scr, lc_scr, sc_scr, h_scr, scan_scr, up_scr, fhist_scr, x1_scr, hn_scr, act_scr,
        *, layer, final, finish_cols):
    tb = TOKEN_BLOCK
    x = x_ref[0]
    hb = _rms(x, g_mix_ref[...]).astype(_BF16)
    zf = jnp.dot(hb, w_in_ref[...], preferred_element_type=_F32)

    def zcol(col0):
        return zf[:, col0:col0 + D_LRU]

    q = (zf[:, _Q0:_Q0 + D_ATTN] * (HEAD_DIM ** -0.5)).astype(_BF16)
    kv_scr[WINDOW:WINDOW + tb, :] = zf[:, _K0:_K0 + 2 * D_KV].astype(_BF16)
    kv_out_ref[0] = zf[tb - WINDOW:tb, _K0:_K0 + 2 * D_KV]

    qi = lax.broadcasted_iota(jnp.int32, (WINDOW, 2 * WINDOW), 0)
    kj = lax.broadcasted_iota(jnp.int32, (WINDOW, 2 * WINDOW), 1)
    band = (kj >= qi) & (kj <= qi + WINDOW)
    band_first = band & (kj >= WINDOW * (1 - jnp.minimum(t, 1)))

    a_blocks = []
    n_sub = tb // WINDOW
    for j in range(n_sub):
        keys = kv_scr[j * WINDOW:(j + 2) * WINDOW, :]
        qj = q[j * WINDOW:(j + 1) * WINDOW, :]
        mask = band_first if j == 0 else band
        outs = []
        for h in range(N_HEADS):
            g = h // GQA_GROUP
            sink = sinks_ref[layer, h]
            s = lax.dot_general(qj[:, h * HEAD_DIM:(h + 1) * HEAD_DIM],
                                keys[:, g * HEAD_DIM:(g + 1) * HEAD_DIM], _NT,
                                preferred_element_type=_F32)
            s = jnp.where(mask, s, -jnp.inf)
            m = jnp.maximum(jnp.max(s, axis=-1, keepdims=True), sink)
            p = jnp.exp(s - m)
            denom = jnp.sum(p, axis=-1, keepdims=True) + jnp.exp(sink - m)
            o = jnp.dot(p.astype(_BF16),
                        keys[:, D_KV + g * HEAD_DIM:D_KV + (g + 1) * HEAD_DIM],
                        preferred_element_type=_F32)
            outs.append(o / denom)
        a_blocks.append(jnp.concatenate(outs, axis=-1))
    a_out = jnp.concatenate(a_blocks, axis=0)

    xc, tail = _causal_conv(zcol(_LX0), lcw_ref, 0, lc_scr)
    xc = xc + lcb_ref[...]
    lbuf_out_ref[0] = tail

    gates = jnp.dot(xc.astype(_BF16), w_gates_ref[...],
                    preferred_element_type=_F32) + b_gates_ref[...]
    a_dec, b_in = _lru_coeffs(xc, gates, lam_ref[...])
    carry = h_scr[0:1, :]
    h_blocks = []
    for j in range(n_sub):
        rows = slice(j * WINDOW, (j + 1) * WINDOW)
        a_cum, b_cum = _scan_rows(a_dec[rows], b_in[rows], scan_scr.at[j])
        hj = a_cum * carry + b_cum
        carry = hj[WINDOW - 1:WINDOW, :]
        h_blocks.append(hj)
        cols = slice(j * (D_MODEL // n_sub), (j + 1) * (D_MODEL // n_sub))
        y_ref[0, :, cols] = finish_cols(cols)
    if final:
        y_ref[0] = _rms(y_ref[0], g_final_ref[...])
    hseq = jnp.concatenate(h_blocks, axis=0)
    h_scr[0:1, :] = carry
    h_out_ref[0] = carry
    b_out = hseq * jax.nn.gelu(zcol(_LG0))

    cu = zcol(_SC0) * zcol(_SU0)
    cconv, tail = _causal_conv(cu, scw_ref, 0, sc_scr)
    sbuf_out_ref[0] = tail
    c_out = zcol(_SB0) * cconv

    g_out = g_out_ref[...]
    merged = jnp.concatenate([
        _rms(a_out, g_out[:, :D_ATTN]),
        _rms(b_out, g_out[:, D_ATTN:D_ATTN + D_LRU]),
        _rms(c_out, g_out[:, D_ATTN + D_LRU:])], axis=-1).astype(_BF16)
    x1 = x + jnp.dot(merged, w_out_ref[...], preferred_element_type=_F32)

    hn_scr[...] = _rms(x1, g_ffn_ref[...]).astype(_BF16)
    x1_scr[...] = x1
    n_t = FF_CHUNK // LANES
    for c in range(N_FF_CHUNKS):
        cu0, cg0 = c * FF_CHUNK, D_FF + c * FF_CHUNK
        buf = c % 2
        up_u = jnp.dot(hn_scr[...], w_up_ref[:, cu0:cu0 + FF_CHUNK], preferred_element_type=_F32)
        up_g = jnp.dot(hn_scr[...], w_up_ref[:, cg0:cg0 + FF_CHUNK], preferred_element_type=_F32)
        uc, tail_u = _causal_conv(up_u, fcw_ref, cu0, up_scr.at[buf, 0:n_t],
                                  hist=fhist_scr.at[c, 0:n_t])
        gc, tail_g = _causal_conv(up_g, fcw_ref, cg0, up_scr.at[buf, n_t:2 * n_t],
                                  hist=fhist_scr.at[c, n_t:2 * n_t])
        fbuf_out_ref[0, :, cu0:cu0 + FF_CHUNK] = tail_u
        fbuf_out_ref[0, :, cg0:cg0 + FF_CHUNK] = tail_g
        act_scr[:, cu0:cu0 + FF_CHUNK] = (jax.nn.silu(gc) * uc).astype(_BF16)


def _layer_spec(a, layer):
    zeros = (0,) * (a.ndim - 1)
    return pl.BlockSpec((None,) + a.shape[1:], lambda *_: (layer,) + zeros,
                        pipeline_mode=pl.Buffered(1))


def _prompt_layer(layer, x, sinks, g_mix, w_in, lcw, lcb, w_gates, b_gates, lam, scw, g_out, w_out,
                  g_ffn, w_up, fcw, w_down, g_final, *, final):
    batch, seq, _ = x.shape
    tb = TOKEN_BLOCK
    stacked = (g_mix, w_in, lcw, lcb, w_gates, b_gates, lam, scw, g_out, w_out, g_ffn, w_up,
               fcw, w_down)
    bps = seq // tb
    n_blocks = batch * bps

    def block(i):
        i = jnp.clip(i, 0, n_blocks - 1)
        return (i // bps, lax.rem(i, bps), 0)

    def seq_of(i):
        return (jnp.minimum(i, n_blocks - 1) // bps, 0, 0)

    in_specs = [pl.BlockSpec(memory_space=pltpu.SMEM),
                pl.BlockSpec((1, tb, D_MODEL), block)]
    in_specs += [_layer_spec(a, layer) for a in stacked]
    in_specs += [pl.BlockSpec(g_final.shape, lambda i: (0, 0), pipeline_mode=pl.Buffered(1))]
    out_shape = (
        jax.ShapeDtypeStruct((batch, seq, D_MODEL), _F32),
        jax.ShapeDtypeStruct((batch, WINDOW, 2 * D_KV), _F32),
        jax.ShapeDtypeStruct((batch, 1, D_LRU), _F32),
        jax.ShapeDtypeStruct((batch, LRU_CONV - 1, D_LRU), _F32),
        jax.ShapeDtypeStruct((batch, SC_CONV - 1, D_SC), _F32),
        jax.ShapeDtypeStruct((batch, FFN_CONV - 1, 2 * D_FF), _F32),
    )
    out_specs = (
        pl.BlockSpec((1, tb, D_MODEL), lambda i: block(i - 1)),
        pl.BlockSpec((1, WINDOW, 2 * D_KV), seq_of),
        pl.BlockSpec((1, 1, D_LRU), seq_of),
        pl.BlockSpec((1, LRU_CONV - 1, D_LRU), seq_of),
        pl.BlockSpec((1, SC_CONV - 1, D_SC), seq_of),
        pl.BlockSpec((1, FFN_CONV - 1, 2 * D_FF), seq_of),
    )
    n_t = 2 * FF_CHUNK // LANES
    scratch = [
        pltpu.VMEM((WINDOW + tb, 2 * D_KV), _BF16),
        pltpu.VMEM((D_LRU // LANES, SUBLANES + tb, LANES), _F32),
        pltpu.VMEM((D_SC // LANES, SUBLANES + tb, LANES), _F32),
        pltpu.VMEM((SUBLANES, D_LRU), _F32),
        pltpu.VMEM((tb // WINDOW, 2, D_LRU // LANES, SUBLANES + WINDOW, LANES), _F32),
        pltpu.VMEM((2, n_t, SUBLANES + tb, LANES), _F32),
        pltpu.VMEM((N_FF_CHUNKS, n_t, SUBLANES, LANES), _F32),
        pltpu.VMEM((tb, D_MODEL), _F32),
        pltpu.VMEM((tb, D_MODEL), _BF16),
        pltpu.VMEM((tb, D_FF), _BF16),
    ]
    return pl.pallas_call(
        functools.partial(_prompt_layer_kernel, layer=layer, final=final, blocks_per_seq=bps,
                          n_blocks=n_blocks),
        grid=(n_blocks + 1,),
        in_specs=in_specs,
        out_specs=out_specs,
        out_shape=out_shape,
        scratch_shapes=scratch,
        compiler_params=pltpu.CompilerParams(
            dimension_semantics=("arbitrary",),
            vmem_limit_bytes=VMEM_LIMIT_BYTES),
        name="prompt_layer",
    )(sinks, x, *stacked, g_final)


def _sample_kernel(
        x_ref, ck_ref, cv_ref, h0_ref, lbuf_ref, sbuf_ref, fst_ref, sink_rows_ref,
        g_mix_ref, w_q_ref, w_in_ref, lcw_ref, lcb_ref, w_gates_ref, b_gates_ref,
        lam_ref, scw_ref, g_out_ref, w_out_ref, g_ffn_ref, w_up_ref, fcw_ref, w_down_ref,
        g_final_ref,
        y_ref, ck_out_ref, cv_out_ref, h_out_ref, lbuf_out_ref, sbuf_out_ref, fst_out_ref,
        xs_scr, hb_scr, q_scr, rest_scr, o_scr, uc_scr, acc_scr,
        *, n_groups):
    l = pl.program_id(0)
    s = pl.program_id(1)
    gs = SEQ_GROUP
    n_blocks = 2 * N_FF_CHUNKS
    last = n_groups + n_blocks - 1

    @pl.when((l == 0) & (s == 0))
    def _():
        xs_scr[...] = x_ref[...]

    @pl.when(s == 0)
    def _():
        hb = _rms(xs_scr[...], g_mix_ref[...]).astype(_BF16)
        hb_scr[...] = hb
        q_scr[...] = jnp.dot(hb, w_q_ref[...], preferred_element_type=_F32) * (HEAD_DIM ** -0.5)
        rest_scr[...] = jnp.dot(hb, w_in_ref[:, D_ATTN:], preferred_element_type=_F32)

    @pl.when(s < n_groups)
    def _():
        row0 = pl.multiple_of(s * gs, gs)
        q_all = jnp.concatenate(
            [q_scr[pl.ds(row0, gs), h * D_KV:(h + 1) * D_KV] for h in range(N_HEADS)], axis=0)
        k_new = rest_scr[pl.ds(row0, gs), 0:D_KV]
        v_new = rest_scr[pl.ds(row0, gs), D_KV:2 * D_KV]
        k_new_rows = jnp.concatenate([k_new] * N_HEADS, axis=0)
        v_new_rows = jnp.concatenate([v_new] * N_HEADS, axis=0)

        keys_t = jnp.concatenate([ck_ref[b] for b in range(gs)], axis=-1).astype(_BF16)
        s_all = jnp.dot(q_all.astype(_BF16), keys_t, preferred_element_type=_F32)
        seq_of_row = lax.broadcasted_iota(jnp.int32, (N_HEADS * gs, WINDOW), 0) & (gs - 1)
        sc = jnp.zeros((N_HEADS * gs, WINDOW), _F32)
        for j in range(gs):
            sc = sc + jnp.where(seq_of_row == j, s_all[:, j * WINDOW:(j + 1) * WINDOW], 0.0)
        s_new = jnp.sum(q_all * k_new_rows, axis=-1, keepdims=True)
        sink = sink_rows_ref[:, 0:1]
        m = jnp.maximum(jnp.maximum(jnp.max(sc, axis=-1, keepdims=True), s_new), sink)
        p = jnp.exp(sc - m)
        p_new = jnp.exp(s_new - m)
        denom = jnp.sum(p, axis=-1, keepdims=True) + p_new + jnp.exp(sink - m)
        p_blk = jnp.concatenate(
            [jnp.where(seq_of_row == j, p, 0.0) for j in range(gs)], axis=-1).astype(_BF16)
        vals_t = jnp.concatenate([cv_ref[b] for b in range(gs)], axis=-1).astype(_BF16)
        o = lax.dot_general(p_blk, vals_t, _NT, preferred_element_type=_F32) + p_new * v_new_rows
        o = o / denom
        row = lax.broadcasted_iota(jnp.int32, (N_HEADS * gs, D_KV), 0)
        lane = lax.broadcasted_iota(jnp.int32, (N_HEADS * gs, D_KV), 1)
        own_kv = (lax.shift_right_logical(row, (GQA_GROUP * gs).bit_length() - 1)
                  == lax.shift_right_logical(lane, HEAD_DIM.bit_length() - 1))
        o = jnp.where(own_kv, o, 0.0)
        for h in range(N_HEADS):
            o_scr[pl.ds(row0, gs), h * D_KV:(h + 1) * D_KV] = o[h * gs:(h + 1) * gs, :]

        kv_new_t = rest_scr[pl.ds(row0, gs), 0:2 * D_KV].T
        pos = lax.broadcasted_iota(jnp.int32, (D_KV, WINDOW), 1)
        for b in range(gs):
            ck_out_ref[b] = jnp.where(pos == WINDOW - 1, kv_new_t[0:D_KV, b:b + 1],
                                      pltpu.roll(ck_ref[b], WINDOW - 1, 1))
            cv_out_ref[b] = jnp.where(pos == WINDOW - 1, kv_new_t[D_KV:2 * D_KV, b:b + 1],
                                      pltpu.roll(cv_ref[b], WINDOW - 1, 1))

    @pl.when(s == n_groups - 1)
    def _():
        rest = rest_scr[...]
        base = 2 * D_KV
        lx = rest[:, base:base + D_LRU]
        lg = rest[:, base + D_LRU:base + 2 * D_LRU]
        sb = rest[:, base + 2 * D_LRU:base + 2 * D_LRU + D_SC]
        scc = rest[:, base + 2 * D_LRU + D_SC:base + 2 * D_LRU + 2 * D_SC]
        su = rest[:, base + 2 * D_LRU + 2 * D_SC:]
        xc = lcb_ref[...] + lcw_ref[LRU_CONV - 1:LRU_CONV, :] * lx
        for k in range(LRU_CONV - 1):
            xc = xc + lcw_ref[k:k + 1, :] * lbuf_ref[k]
        for k in range(LRU_CONV - 2):
            lbuf_out_ref[k] = lbuf_ref[k + 1]
        lbuf_out_ref[LRU_CONV - 2] = lx
        gates = jnp.dot(xc.astype(_BF16), w_gates_ref[...],
                        preferred_element_type=_F32) + b_gates_ref[...]
        a_dec, b_in = _lru_coeffs(xc, gates, lam_ref[...])
        h_new = a_dec * h0_ref[...] + b_in
        h_out_ref[...] = h_new
        b_out = h_new * jax.nn.gelu(lg)
        cu = scc * su
        cconv = scw_ref[SC_CONV - 1:SC_CONV, :] * cu
        for k in range(SC_CONV - 1):
            cconv = cconv + scw_ref[k:k + 1, :] * sbuf_ref[:, k, :]
        for k in range(SC_CONV - 2):
            sbuf_out_ref[:, k, :] = sbuf_ref[:, k + 1, :]
        sbuf_out_ref[:, SC_CONV - 2, :] = cu
        c_out = sb * cconv
        g_out = g_out_ref[...]
        o_pad = o_scr[...]
        a_scale = lax.rsqrt(jnp.sum(o_pad * o_pad, axis=-1, keepdims=True) / D_ATTN + EPS)
        merged = jnp.concatenate([
            o_pad * a_scale * g_out[:, :D_QPAD],
            _rms(b_out, g_out[:, D_QPAD:D_QPAD + D_LRU]),
            _rms(c_out, g_out[:, D_QPAD + D_LRU:])], axis=-1).astype(_BF16)
        x1 = xs_scr[...] + jnp.dot(merged, w_out_ref[...], preferred_element_type=_F32)
        hb_scr[...] = _rms(x1, g_ffn_ref[...]).astype(_BF16)
        acc_scr[...] = x1

    @pl.when(s >= n_groups)
    def _():
        j = s - n_groups
        up = jnp.dot(hb_scr[...], w_up_ref[...], preferred_element_type=_F32)
        upc = fcw_ref[FFN_CONV - 1:FFN_CONV, :] * up
        for k in range(FFN_CONV - 1):
            upc = upc + fcw_ref[k:k + 1, :] * fst_ref[:, k, :]
        for k in range(FFN_CONV - 2):
            fst_out_ref[:, k, :] = fst_ref[:, k + 1, :]
        fst_out_ref[:, FFN_CONV - 2, :] = up

        @pl.when(j < N_FF_CHUNKS)
        def _():
            uc_scr[j] = upc

        @pl.when(j >= N_FF_CHUNKS)
        def _():
            act = (jax.nn.silu(upc) * uc_scr[j - N_FF_CHUNKS]).astype(_BF16)
            acc_scr[...] += jnp.dot(act, w_down_ref[...], preferred_element_type=_F32)

    @pl.when(s == last)
    def _():
        xs_scr[...] = acc_scr[...]

    @pl.when((s == last) & (l == DEPTH - 1))
    def _():
        y_ref[...] = _rms(acc_scr[...], g_final_ref[...])


def _sample_step(x, ck_t, cv_t, h0, lbuf, sbuf, fstate, sink_rows, g_mix, w_q, w_in, lcw,
                 lcb, w_gates, b_gates, lam, scw, g_out_pad, w_out_pad, g_ffn, w_up, fcw, w_down,
                 g_final):
    n_seq = x.shape[0]
    gs = SEQ_GROUP
    n_groups = n_seq // gs
    n_blocks = 2 * N_FF_CHUNKS
    steps = n_groups + n_blocks

    def per_layer(a):
        zeros = (0,) * (a.ndim - 1)
        return pl.BlockSpec((None,) + a.shape[1:], lambda l, s: (l,) + zeros)

    def group(l, s):
        return (l, jnp.minimum(s, n_groups - 1), 0, 0)

    def ff_block(l, s):
        return jnp.clip(s - n_groups, 0, n_blocks - 1)

    cache_spec = pl.BlockSpec((None, gs, D_KV, WINDOW), group)
    fst_spec = pl.BlockSpec((None, n_seq, FFN_CONV - 1, FF_CHUNK), lambda l, s: (l, 0, 0, ff_block(l, s)))
    in_specs = [
        pl.BlockSpec(x.shape, lambda l, s: (0, 0)),
        cache_spec, cache_spec, per_layer(h0), per_layer(lbuf), per_layer(sbuf), fst_spec,
        per_layer(sink_rows), per_layer(g_mix), per_layer(w_q), per_layer(w_in),
        per_layer(lcw), per_layer(lcb), per_layer(w_gates), per_layer(b_gates), per_layer(lam),
        per_layer(scw), per_layer(g_out_pad), per_layer(w_out_pad), per_layer(g_ffn),
        pl.BlockSpec((None, D_MODEL, FF_CHUNK), lambda l, s: (l, 0, ff_block(l, s))),
        pl.BlockSpec((None, FFN_CONV, FF_CHUNK), lambda l, s: (l, 0, ff_block(l, s))),
        pl.BlockSpec((None, FF_CHUNK, D_MODEL),
                     lambda l, s: (l, jnp.clip(s - n_groups - N_FF_CHUNKS, 0, N_FF_CHUNKS - 1), 0)),
        pl.BlockSpec(g_final.shape, lambda l, s: (0, 0)),
    ]
    out_shape = (
        jax.ShapeDtypeStruct((n_seq, D_MODEL), _F32),
        jax.ShapeDtypeStruct(ck_t.shape, _F32),
        jax.ShapeDtypeStruct(cv_t.shape, _F32),
        jax.ShapeDtypeStruct(h0.shape, _F32),
        jax.ShapeDtypeStruct(lbuf.shape, _F32),
        jax.ShapeDtypeStruct(sbuf.shape, _F32),
        jax.ShapeDtypeStruct(fstate.shape, _F32),
    )
    out_specs = (
        pl.BlockSpec((n_seq, D_MODEL), lambda l, s: (0, 0)),
        cache_spec, cache_spec, per_layer(h0), per_layer(lbuf), per_layer(sbuf), fst_spec,
    )
    scratch = [
        pltpu.VMEM((n_seq, D_MODEL), _F32),
        pltpu.VMEM((n_seq, D_MODEL), _BF16),
        pltpu.VMEM((n_seq, D_QPAD), _F32),
        pltpu.VMEM((n_seq, D_IN - D_ATTN), _F32),
        pltpu.VMEM((n_seq, D_QPAD), _F32),
        pltpu.VMEM((N_FF_CHUNKS, n_seq, FF_CHUNK), _F32),
        pltpu.VMEM((n_seq, D_MODEL), _F32),
    ]
    return pl.pallas_call(
        functools.partial(_sample_kernel, n_groups=n_groups),
        grid=(DEPTH, steps),
        in_specs=in_specs,
        out_specs=out_specs,
        out_shape=out_shape,
        scratch_shapes=scratch,
        compiler_params=pltpu.CompilerParams(
            dimension_semantics=("arbitrary", "arbitrary"),
            vmem_limit_bytes=VMEM_LIMIT_BYTES),
        name="sample_step",
    )(x, ck_t, cv_t, h0, lbuf, sbuf, fstate, sink_rows, g_mix, w_q, w_in, lcw, lcb, w_gates,
      b_gates, lam, scw, g_out_pad, w_out_pad, g_ffn, w_up, fcw, w_down, g_final)


def _block_diag(w):
    eye = jnp.eye(N_LRU_BLOCKS, dtype=w.dtype).reshape(1, N_LRU_BLOCKS, 1, N_LRU_BLOCKS, 1)
    return (w[:, :, :, None, :] * eye).reshape(DEPTH, D_LRU, D_LRU)


def _pad_heads(w, axis):
    pieces = []
    for h in range(N_HEADS):
        blk = lax.slice_in_dim(w, h * HEAD_DIM, (h + 1) * HEAD_DIM, axis=axis)
        for g in range(N_KV_HEADS):
            pieces.append(blk if g == h // GQA_GROUP else jnp.zeros_like(blk))
    return jnp.concatenate(pieces, axis=axis)


def kernel(x_prompt, x_sample, cache_k, cache_v, state_rglru, state_lru_conv, state_sconv, state_ffn_conv, g_mix, w_in, sinks, lru_conv_w, lru_conv_b, lru_wa, lru_ba, lru_wi, lru_bi, lru_lambda, sc_conv_w, g_out, w_out, g_ffn, w_up, ffn_conv_w, w_down, g_final):
    batch = x_prompt.shape[0]
    n_seq = x_sample.shape[0]

    w_in_b = w_in.astype(_BF16)
    w_out_b = w_out.astype(_BF16)
    w_up_b = w_up.astype(_BF16)
    w_down_b = w_down.astype(_BF16)
    w_gates = jnp.concatenate([_block_diag(lru_wa), _block_diag(lru_wi)], axis=-1).astype(_BF16)
    b_gates = jnp.concatenate([lru_ba, lru_bi], axis=-1)[:, None, :]
    w_q_pad = _pad_heads(w_in[:, :, :D_ATTN], 2).astype(_BF16)
    w_out_pad = jnp.concatenate([_pad_heads(w_out[:, :D_ATTN], 1), w_out[:, D_ATTN:]],
                                axis=1).astype(_BF16)
    g_out_pad = jnp.concatenate([_pad_heads(g_out[:, :D_ATTN], 1), g_out[:, D_ATTN:]], axis=1)
    sink_rows = jnp.broadcast_to(
        jnp.repeat(sinks, SEQ_GROUP, axis=1)[:, :, None], (DEPTH, N_HEADS * SEQ_GROUP, D_KV))

    row = lambda a: a[:, None, :]
    g_mix_r, g_out_r, g_ffn_r = row(g_mix), row(g_out), row(g_ffn)
    lcb_r, lam_r, g_out_pad_r = row(lru_conv_b), row(lru_lambda), row(g_out_pad)
    g_final_r = g_final[None, :]

    ck_t = jnp.transpose(cache_k, (0, 1, 3, 4, 2)).reshape(DEPTH, n_seq, D_KV, WINDOW)
    cv_t = jnp.transpose(cache_v, (0, 1, 3, 4, 2)).reshape(DEPTH, n_seq, D_KV, WINDOW)
    lbuf_t = jnp.transpose(state_lru_conv, (0, 2, 1, 3))
    ys, ck_new, cv_new, h_new, lbuf_new, sbuf_new, fst_new = _sample_step(
        x_sample.reshape(n_seq, D_MODEL), ck_t, cv_t, state_rglru, lbuf_t, state_sconv,
        state_ffn_conv, sink_rows, g_mix_r, w_q_pad, w_in_b, lru_conv_w, lcb_r, w_gates,
        b_gates, lam_r, sc_conv_w, g_out_pad_r, w_out_pad, g_ffn_r, w_up_b, ffn_conv_w, w_down_b,
        g_final_r)
    unview = lambda c: jnp.transpose(
        c.reshape(DEPTH, n_seq, N_KV_HEADS, HEAD_DIM, WINDOW), (0, 1, 4, 2, 3))
    s_out = (unview(ck_new), unview(cv_new), h_new, jnp.transpose(lbuf_new, (0, 2, 1, 3)),
             sbuf_new, fst_new)

    xp = x_prompt
    p_states = []
    for l in range(DEPTH):
        xp, kv_new, h_last, lbuf_p, sbuf_p, fbuf_p = _prompt_layer(
            l, xp, sinks, g_mix_r, w_in_b, lru_conv_w, lcb_r, w_gates, b_gates, lam_r, sc_conv_w,
            g_out_r, w_out_b, g_ffn_r, w_up_b, ffn_conv_w, w_down_b, g_final_r,
            final=(l == DEPTH - 1))
        p_states.append((
            kv_new[:, :, :D_KV].reshape(batch, WINDOW, N_KV_HEADS, HEAD_DIM),
            kv_new[:, :, D_KV:].reshape(batch, WINDOW, N_KV_HEADS, HEAD_DIM),
            h_last.reshape(batch, D_LRU), lbuf_p, sbuf_p, fbuf_p))

    stack = lambda i: jnp.stack([st[i] for st in p_states])
    return ((xp, ys.reshape(n_seq, 1, D_MODEL)) + tuple(stack(i) for i in range(6)) + s_out)
```
